```python
import math
import jax, jax.numpy as jnp
from jax import lax
import numpy as np

D_MODEL = 2048
BATCH = 4
SEQ = 4096
DEPTH = 4

GRID_W = 64
CTX_LEN = 256
HEAD_DIM = 128
MIX_WIDTH = D_MODEL
A_HEADS = MIX_WIDTH // 2 // HEAD_DIM
A_KV_HEADS = 2
A_QW = A_HEADS * HEAD_DIM
A_KVW = A_KV_HEADS * HEAD_DIM
WINDOW = 128
WBLK = WINDOW
B_CH = MIX_WIDTH // 2
CONV_K = 31
AB_IN = A_QW + 2 * A_KVW + 2 * B_CH
NA_HEADS = MIX_WIDTH // HEAD_DIM
NA_W = NA_HEADS * HEAD_DIM
NAT_KH = 8
NAT_KW = 16
N_EXPERTS = 16
N_GROUPS = 4
EXPERTS_PER_GROUP = N_EXPERTS // N_GROUPS
TOP_K = 2
D_EXPERT = 1536
MOE_BLK = 256
ROPE_BASE = 10000.0
EPS = 1e-6
NEG_INF = -1e30

kernel_name = "hybrid_dit_window_conformer_natten_moe"


def rms_norm(x, g):
    xf = x.astype(jnp.float32)
    y = xf * lax.rsqrt(jnp.mean(xf * xf, axis=-1, keepdims=True) + EPS)
    return (y * g.astype(jnp.float32)).astype(x.dtype)


def layer_norm(x, g, b):
    xf = x.astype(jnp.float32)
    mu = jnp.mean(xf, axis=-1, keepdims=True)
    var = jnp.mean(jnp.square(xf - mu), axis=-1, keepdims=True)
    y = (xf - mu) * lax.rsqrt(var + EPS)
    return (y * g.astype(jnp.float32) + b.astype(jnp.float32)).astype(x.dtype)


def modulate(h, shift, scale):
    return h * (1 + scale) + shift


def axial_rope(n_tok, dtype):
    nf = HEAD_DIM // 4
    inv = jnp.power(ROPE_BASE, -jnp.arange(nf, dtype=jnp.float32) / nf)
    t = jnp.arange(n_tok)
    ar = (t // GRID_W).astype(jnp.float32)[:, None] * inv
    ac = (t % GRID_W).astype(jnp.float32)[:, None] * inv
    ang = jnp.concatenate([ar, ar, ac, ac], axis=-1)[:, None, :]
    return jnp.cos(ang).astype(dtype), jnp.sin(ang).astype(dtype)


def rotate_half(u):
    a, b = jnp.split(u, 2, axis=-1)
    return jnp.concatenate([-b, a], axis=-1)


def apply_rope(x, cos, sin):
    xr, xc = jnp.split(x, 2, axis=-1)
    return x * cos + jnp.concatenate([rotate_half(xr), rotate_half(xc)], axis=-1) * sin


def dense_ctx_attn(q, k, v, sink=None):
    B, L, H, HD = q.shape
    hkv = k.shape[2]
    G = H // hkv
    qg = q.reshape(B, L, hkv, G, HD)
    s = jnp.einsum('blhgd,bmhd->bhglm', qg, k).astype(jnp.float32) * (HD ** -0.5)
    if sink is not None:
        sk = jnp.broadcast_to(sink.astype(jnp.float32).reshape(hkv, G)[None, :, :, None, None], (B, hkv, G, L, 1))
        s = jnp.concatenate([s, sk], axis=-1)
    p = jax.nn.softmax(s, axis=-1)[..., :L]
    o = jnp.einsum('bhglm,bmhd->blhgd', p.astype(v.dtype), v)
    return o.reshape(B, L, H * HD)


def window_attn(q, k, v, kc, vc, sink):
    B, S, H, HD = q.shape
    hkv = k.shape[2]
    G = H // hkv
    L = kc.shape[1]
    nb = S // WBLK
    qb = q.reshape(B, nb, WBLK, hkv, G, HD)
    pad = ((0, 0), (WBLK, WBLK), (0, 0), (0, 0))

    def band(t):
        tp = jnp.pad(t, pad).reshape(B, nb + 2, WBLK, hkv, HD)
        return jnp.concatenate([tp[:, :-2], tp[:, 1:-1], tp[:, 2:]], axis=2)

    kb, vb = band(k), band(v)
    scale = HD ** -0.5
    s_loc = jnp.einsum('bnqhgd,bnkhd->bnhgqk', qb, kb).astype(jnp.float32) * scale
    qi = jnp.arange(WBLK)[:, None]
    kj = jnp.arange(3 * WBLK)[None, :]
    in_win = jnp.abs(kj - WBLK - qi) <= WINDOW
    key_pos = (jnp.arange(nb)[:, None, None] - 1) * WBLK + kj[None]
    mask = in_win[None] & (key_pos >= 0) & (key_pos < S)
    s_loc = jnp.where(mask[None, :, None, None], s_loc, NEG_INF)
    s_ctx = jnp.einsum('bnqhgd,blhd->bnhgql', qb, kc).astype(jnp.float32) * scale
    sk = jnp.broadcast_to(sink.astype(jnp.float32).reshape(hkv, G)[None, None, :, :, None, None], (B, nb, hkv, G, WBLK, 1))
    p = jax.nn.softmax(jnp.concatenate([s_loc, s_ctx, sk], axis=-1), axis=-1).astype(v.dtype)
    nk = 3 * WBLK
    o = (jnp.einsum('bnhgqk,bnkhd->bnqhgd', p[..., :nk], vb)
         + jnp.einsum('bnhgql,blhd->bnqhgd', p[..., nk:nk + L], vc))
    return o.reshape(B, S, H * HD)


def conformer_conv(u, w, b, g, beta):
    a, gt = jnp.split(u, 2, axis=-1)
    h = a * jax.nn.sigmoid(gt)
    h = lax.conv_general_dilated(h, w[:, None, :].astype(h.dtype), window_strides=(1,),
                                 padding=[(CONV_K // 2, CONV_K // 2)],
                                 dimension_numbers=('NWC', 'WIO', 'NWC'),
                                 feature_group_count=h.shape[-1]) + b
    return jax.nn.silu(layer_norm(h, g, beta))


def na_attn(q, k, v, kc, vc, rpb):
    B, S, H, HD = q.shape
    rows = S // GRID_W
    kh = min(NAT_KH, rows)
    kw = NAT_KW
    L = kc.shape[1]
    qg = q.reshape(B, rows, GRID_W, H, HD)
    kg = k.reshape(B, rows, GRID_W, H, HD)
    vg = v.reshape(B, rows, GRID_W, H, HD)
    cidx = jnp.arange(GRID_W)
    cs = jnp.clip(cidx - kw // 2, 0, GRID_W - kw)
    col_mask = (cidx[None, :] >= cs[:, None]) & (cidx[None, :] < cs[:, None] + kw)
    dc_idx = jnp.clip(cidx[None, :] - cidx[:, None], -(NAT_KW - 1), NAT_KW - 1) + NAT_KW - 1
    rpb_cols = rpb.astype(jnp.float32)[:, :, dc_idx]
    scale = HD ** -0.5

    def one_row(r):
        rs = jnp.clip(r - kh // 2, 0, rows - kh)
        q_r = lax.dynamic_index_in_dim(qg, r, axis=1, keepdims=False)
        k_blk = lax.dynamic_slice_in_dim(kg, rs, kh, axis=1)
        v_blk = lax.dynamic_slice_in_dim(vg, rs, kh, axis=1)
        dr_idx = rs + jnp.arange(kh) - r + NAT_KH - 1
        bias = jnp.take(rpb_cols, dr_idx, axis=1).transpose(0, 2, 1, 3)
        s = jnp.einsum('bqhd,brkhd->bhqrk', q_r, k_blk).astype(jnp.float32) * scale + bias
        s = jnp.where(col_mask[None, None, :, None, :], s, NEG_INF).reshape(B, H, GRID_W, kh * GRID_W)
        s_c = jnp.einsum('bqhd,blhd->bhql', q_r, kc).astype(jnp.float32) * scale
        p = jax.nn.softmax(jnp.concatenate([s, s_c], axis=-1), axis=-1).astype(v.dtype)
        nk = kh * GRID_W
        return (jnp.einsum('bhqk,bkhd->bqhd', p[..., :nk], v_blk.reshape(B, nk, H, HD))
                + jnp.einsum('bhql,blhd->bqhd', p[..., nk:nk + L], vc))

    o = lax.map(one_row, jnp.arange(rows))
    return o.transpose(1, 0, 2, 3, 4).reshape(B, S, H * HD)


def mixer_ab(hx, hy, w_in, w_out, q_g, k_g, sink, conv_w, conv_b, ln_g, ln_b, cos, sin, with_ctx):
    B, S, _ = hx.shape
    L = hy.shape[1]
    k0, v0, u0 = A_QW, A_QW + A_KVW, A_QW + 2 * A_KVW
    px = hx @ w_in
    q = apply_rope(rms_norm(px[..., :k0].reshape(B, S, A_HEADS, HEAD_DIM), q_g), cos, sin)
    k = apply_rope(rms_norm(px[..., k0:v0].reshape(B, S, A_KV_HEADS, HEAD_DIM), k_g), cos, sin)
    v = px[..., v0:u0].reshape(B, S, A_KV_HEADS, HEAD_DIM)
    py = hy @ w_in if with_ctx else hy @ w_in[:, k0:u0]
    kv_y = py[..., k0:u0] if with_ctx else py
    kc = rms_norm(kv_y[..., :A_KVW].reshape(B, L, A_KV_HEADS, HEAD_DIM), k_g)
    vc = kv_y[..., A_KVW:].reshape(B, L, A_KV_HEADS, HEAD_DIM)
    att_x = window_attn(q, k, v, kc, vc, sink)
    conv_x = conformer_conv(px[..., u0:], conv_w, conv_b, ln_g, ln_b)
    ox = jnp.concatenate([att_x, conv_x], axis=-1) @ w_out
    if not with_ctx:
        return ox, None
    qc = rms_norm(py[..., :k0].reshape(B, L, A_HEADS, HEAD_DIM), q_g)
    att_y = dense_ctx_attn(qc, kc, vc, sink)
    conv_y = conformer_conv(py[..., u0:], conv_w, conv_b, ln_g, ln_b)
    oy = jnp.concatenate([att_y, conv_y], axis=-1) @ w_out
    return ox, oy


def mixer_na(hx, hy, w_in, w_out, q_g, k_g, rpb, with_ctx):
    B, S, _ = hx.shape
    L = hy.shape[1]
    px = hx @ w_in
    q = rms_norm(px[..., :NA_W].reshape(B, S, NA_HEADS, HEAD_DIM), q_g)
    k = rms_norm(px[..., NA_W:2 * NA_W].reshape(B, S, NA_HEADS, HEAD_DIM), k_g)
    v = px[..., 2 * NA_W:].reshape(B, S, NA_HEADS, HEAD_DIM)
    py = hy @ w_in if with_ctx else hy @ w_in[:, NA_W:]
    kv_y = py[..., NA_W:] if with_ctx else py
    kc = rms_norm(kv_y[..., :NA_W].reshape(B, L, NA_HEADS, HEAD_DIM), k_g)
    vc = kv_y[..., NA_W:].reshape(B, L, NA_HEADS, HEAD_DIM)
    ox = na_attn(q, k, v, kc, vc, rpb) @ w_out
    if not with_ctx:
        return ox, None
    qc = rms_norm(py[..., :NA_W].reshape(B, L, NA_HEADS, HEAD_DIM), q_g)
    oy = dense_ctx_attn(qc, kc, vc) @ w_out
    return ox, oy


def moe_ffn(h, router_w, router_b, w_gate, w_up, w_down):
    T, D = h.shape
    aff = jax.nn.sigmoid(h.astype(jnp.float32) @ router_w.astype(jnp.float32))
    sel = (aff + router_b.astype(jnp.float32)).reshape(T, N_GROUPS, EXPERTS_PER_GROUP)
    group_score = lax.top_k(sel, TOP_K)[0].sum(-1)
    grp = jnp.argmax(group_score, axis=-1)
    sel_in = jnp.take_along_axis(sel, grp[:, None, None], axis=1)[:, 0]
    _, local = lax.top_k(sel_in, TOP_K)
    expert = grp[:, None] * EXPERTS_PER_GROUP + local
    gate = jnp.take_along_axis(aff, expert, axis=1)
    gate = gate / jnp.sum(gate, axis=-1, keepdims=True)
    n_assign = T * TOP_K
    e_flat = expert.reshape(-1)
    tok_flat = jnp.repeat(jnp.arange(T), TOP_K)
    w_flat = gate.reshape(-1)
    order = jnp.argsort(e_flat)
    e_s, tok_s, w_s = e_flat[order], tok_flat[order], w_flat[order]
    counts = jnp.bincount(e_flat, length=N_EXPERTS)
    starts = jnp.cumsum(counts) - counts
    padded = (counts + MOE_BLK - 1) // MOE_BLK * MOE_BLK
    pends = jnp.cumsum(padded)
    pstarts = pends - padded
    pos = pstarts[e_s] + jnp.arange(n_assign) - starts[e_s]
    n_blocks = -(-(n_assign + N_EXPERTS * (MOE_BLK - 1)) // MOE_BLK)
    buf = jnp.zeros((n_blocks * MOE_BLK, D), h.dtype).at[pos].set(h[tok_s])
    block_expert = jnp.minimum(jnp.searchsorted(pends, jnp.arange(n_blocks) * MOE_BLK, side='right'), N_EXPERTS - 1)

    def expert_block(args):
        xb, e = args
        return (jax.nn.silu(xb @ w_gate[e]) * (xb @ w_up[e])) @ w_down[e]

    y = lax.map(expert_block, (buf.reshape(n_blocks, MOE_BLK, D), block_expert)).reshape(-1, D)
    return jnp.zeros((T, D), h.dtype).at[tok_s].add(y[pos] * w_s[:, None].astype(h.dtype))


def setup_inputs(seed: int = 0) -> dict:
    key = jax.random.key(seed)
    ks = jax.random.split(key, 26)
    n_even = (DEPTH + 1) // 2
    n_odd = DEPTH // 2
    D = D_MODEL

    def nrm(k, shape, scale):
        return jax.random.normal(k, shape, jnp.float32) * scale

    return {
        'x': nrm(ks[0], (BATCH, SEQ, D), 1.0),
        'c': nrm(ks[1], (BATCH, D), 1.0),
        'ctx': nrm(ks[2], (BATCH, CTX_LEN, D), 1.0),
        'c_ctx': nrm(ks[3], (D,), 1.0),
        'ada_w': nrm(ks[4], (DEPTH, D, 6 * D), 0.5 * D ** -0.5),
        'ada_b': nrm(ks[5], (DEPTH, 6 * D), 0.02),
        'norm_mix_g': 1.0 + nrm(ks[6], (DEPTH, D), 0.02),
        'norm_ffn_g': 1.0 + nrm(ks[7], (DEPTH, D), 0.02),
        'ab_w_in': nrm(ks[8], (n_even, D, AB_IN), D ** -0.5),
        'ab_w_out': nrm(ks[9], (n_even, A_QW + B_CH, D), (A_QW + B_CH) ** -0.5),
        'ab_q_norm': 1.0 + nrm(ks[10], (n_even, HEAD_DIM), 0.02),
        'ab_k_norm': 1.0 + nrm(ks[11], (n_even, HEAD_DIM), 0.02),
        'ab_sink': nrm(ks[12], (n_even, A_HEADS), 0.5),
        'conv_w': nrm(ks[13], (n_even, CONV_K, B_CH), CONV_K ** -0.5),
        'conv_b': nrm(ks[14], (n_even, B_CH), 0.02),
        'conv_ln_g': 1.0 + nrm(ks[15], (n_even, B_CH), 0.02),
        'conv_ln_b': nrm(ks[16], (n_even, B_CH), 0.02),
        'na_w_in': nrm(ks[17], (n_odd, D, 3 * NA_W), D ** -0.5),
        'na_w_out': nrm(ks[18], (n_odd, NA_W, D), NA_W ** -0.5),
        'na_q_norm': 1.0 + nrm(ks[19], (n_odd, HEAD_DIM), 0.02),
        'na_k_norm': 1.0 + nrm(ks[20], (n_odd, HEAD_DIM), 0.02),
        'na_rpb': nrm(ks[21], (n_odd, NA_HEADS, 2 * NAT_KH - 1, 2 * NAT_KW - 1), 0.1),
        'router_w': nrm(ks[22], (D, N_EXPERTS), D ** -0.5),
        'router_b': nrm(ks[23], (N_EXPERTS,), 0.01),
        'moe_w_gate': nrm(ks[24], (DEPTH, N_EXPERTS, D, D_EXPERT), D ** -0.5),
        'moe_w_up': nrm(jax.random.fold_in(ks[24], 1), (DEPTH, N_EXPERTS, D, D_EXPERT), D ** -0.5),
        'moe_w_down': nrm(ks[25], (DEPTH, N_EXPERTS, D_EXPERT, D), D_EXPERT ** -0.5),
    }


def reference(x, c, ctx, c_ctx, ada_w, ada_b, norm_mix_g, norm_ffn_g,
              ab_w_in, ab_w_out, ab_q_norm, ab_k_norm, ab_sink,
              conv_w, conv_b, conv_ln_g, conv_ln_b,
              na_w_in, na_w_out, na_q_norm, na_k_norm, na_rpb,
              router_w, router_b, moe_w_gate, moe_w_up, moe_w_down):
    B, S, D = x.shape
    L = ctx.shape[1]
    cos, sin = axial_rope(S, x.dtype)
    sc = jax.nn.silu(c)
    scc = jax.nn.silu(c_ctx)
    y = ctx
    for i in range(DEPTH):
        with_ctx = i < DEPTH - 1
        j = i // 2
        mx = jnp.split((sc @ ada_w[i] + ada_b[i])[:, None, :], 6, axis=-1)
        my = jnp.split(scc @ ada_w[i] + ada_b[i], 6, axis=-1)
        hx = modulate(rms_norm(x, norm_mix_g[i]), mx[0], mx[1])
        hy = modulate(rms_norm(y, norm_mix_g[i]), my[0], my[1])
        if i % 2 == 0:
            ox, oy = mixer_ab(hx, hy, ab_w_in[j], ab_w_out[j], ab_q_norm[j], ab_k_norm[j], ab_sink[j],
                              conv_w[j], conv_b[j], conv_ln_g[j], conv_ln_b[j], cos, sin, with_ctx)
        else:
            ox, oy = mixer_na(hx, hy, na_w_in[j], na_w_out[j], na_q_norm[j], na_k_norm[j], na_rpb[j], with_ctx)
        x = x + mx[2] * ox
        hx = modulate(rms_norm(x, norm_ffn_g[i]), mx[3], mx[4])
        if with_ctx:
            y = y + my[2] * oy
            hy = modulate(rms_norm(y, norm_ffn_g[i]), my[3], my[4])
            tok = jnp.concatenate([hx.reshape(B * S, D), hy.reshape(B * L, D)], axis=0)
            f = moe_ffn(tok, router_w, router_b, moe_w_gate[i], moe_w_up[i], moe_w_down[i])
            x = x + mx[5] * f[:B * S].reshape(B, S, D)
            y = y + my[5] * f[B * S:].reshape(B, L, D)
        else:
            f = moe_ffn(hx.reshape(B * S, D), router_w, router_b, moe_w_gate[i], moe_w_up[i], moe_w_down[i])
            x = x + mx[5] * f.reshape(B, S, D)
    return x
```

```python
import functools

import numpy as np
import jax
import jax.numpy as jnp
from jax import lax
from jax.experimental import pallas as pl
from jax.experimental.pallas import tpu as pltpu

F32 = jnp.float32
BF16 = jnp.bfloat16

HEAD_DIM = 128
LANES = 128
GRID_W = 64
WINDOW = 128
N_GROUPS = 4
TOP_K = 2
ROPE_BASE = 10000.0
EPS = 1e-6
NEG_INF = -1e30
MOE_BLK = 256
NA_RB = 4
HALO = 16
VMEM_LIMIT = 56 * 1024 * 1024


def _cparams(sem, vmem=None):
    return pltpu.CompilerParams(dimension_semantics=sem, vmem_limit_bytes=vmem)


def _silu(v):
    return v * jax.nn.sigmoid(v)


def _rms(v, g):
    ms = jnp.mean(v * v, axis=-1, keepdims=True)
    return v * lax.rsqrt(ms + EPS) * g


def _ada_kernel(c_ref, w_ref, b_ref, o_ref):
    sc = _silu(c_ref[...])
    o_ref[0] = jnp.dot(sc.astype(BF16), w_ref[0].astype(BF16), preferred_element_type=F32) + b_ref[0]


def _ada_all(c8, ada_w, ada_b):
    depth, d, n = ada_w.shape
    tn = min(n, 1024)
    return pl.pallas_call(
        _ada_kernel,
        grid=(depth, n // tn),
        in_specs=[pl.BlockSpec((8, d), lambda l, j: (0, 0)),
                  pl.BlockSpec((1, d, tn), lambda l, j: (l, 0, j)),
                  pl.BlockSpec((1, 1, tn), lambda l, j: (l, 0, j))],
        out_specs=pl.BlockSpec((1, 8, tn), lambda l, j: (l, 0, j)),
        out_shape=jax.ShapeDtypeStruct((depth, 8, n), F32),
        compiler_params=_cparams(("arbitrary", "arbitrary"), VMEM_LIMIT),
        name="ada_mod",
    )(c8, ada_w, ada_b.reshape(depth, 1, n))


def _rope(y, cos, sin):
    lane = lax.broadcasted_iota(jnp.int32, y.shape, 1)
    first = (lane & 32) == 0
    fwd = pltpu.roll(y, 32, 1)
    bwd = pltpu.roll(y, 96, 1)
    return y * cos + jnp.where(first, -bwd, fwd) * sin


def _inproj_kernel(*refs, d, tn, groups, rope):
    if rope:
        x_ref, mod_ref, g_ref, w_ref, qg_ref, kg_ref, cos_ref, sin_ref, o_ref, h_scr = refs
    else:
        x_ref, mod_ref, g_ref, w_ref, qg_ref, kg_ref, o_ref, h_scr = refs
        cos_ref = sin_ref = None
    j = pl.program_id(1)

    @pl.when(j == 0)
    def _():
        y = _rms(x_ref[...], g_ref[...])
        shift = mod_ref[0, :, 0:d]
        scale = mod_ref[0, :, d:2 * d]
        h_scr[...] = (y * (1.0 + scale) + shift).astype(BF16)

    acc = jnp.dot(h_scr[...], w_ref[...], preferred_element_type=F32)

    for kinds, lo, hi in groups:
        @pl.when((j >= lo) & (j <= hi))
        def _(kinds=kinds):
            if all(k == "p" for k in kinds):
                o_ref[...] = acc.astype(o_ref.dtype)
                return
            for s, kind in enumerate(kinds):
                piece = acc[:, s * LANES:(s + 1) * LANES]
                if kind != "p":
                    piece = _rms(piece, (qg_ref if kind == "q" else kg_ref)[...])
                    if rope:
                        piece = _rope(piece, cos_ref[...], sin_ref[...])
                o_ref[:, s * LANES:(s + 1) * LANES] = piece.astype(o_ref.dtype)


def _inproj(tok, mod3, layer, norm_g, w, qg, kg, kind_of_col, cos, sin, *, n_x_tiles, tiles_per_seq, tm, tn):
    t, d = tok.shape
    n = w.shape[1]
    nj = n // tn
    per_j = [tuple(kind_of_col(j * tn + s * LANES) for s in range(tn // LANES)) for j in range(nj)]
    groups = []
    for j, kinds in enumerate(per_j):
        if groups and groups[-1][0] == kinds and groups[-1][2] == j - 1:
            groups[-1] = (kinds, groups[-1][1], j)
        else:
            groups.append((kinds, j, j))
    rope = cos is not None
    n_batch = n_x_tiles // tiles_per_seq

    def mod_idx(i, j):
        return (layer * 8 + jnp.where(i < n_x_tiles, i // tiles_per_seq, n_batch), 0, 0)

    def pos_idx(i, j):
        return (jnp.where(i < n_x_tiles, i % tiles_per_seq, tiles_per_seq), 0)

    in_specs = [pl.BlockSpec((tm, d), lambda i, j: (i, 0)),
                pl.BlockSpec((1, 1, mod3.shape[2]), mod_idx),
                pl.BlockSpec((1, d), lambda i, j: (0, 0)),
                pl.BlockSpec((d, tn), lambda i, j: (0, j)),
                pl.BlockSpec((1, HEAD_DIM), lambda i, j: (0, 0)),
                pl.BlockSpec((1, HEAD_DIM), lambda i, j: (0, 0))]
    args = [tok, mod3, norm_g.reshape(1, d), w, qg.reshape(1, HEAD_DIM), kg.reshape(1, HEAD_DIM)]
    if rope:
        in_specs += [pl.BlockSpec((tm, HEAD_DIM), pos_idx), pl.BlockSpec((tm, HEAD_DIM), pos_idx)]
        args += [cos, sin]
    return pl.pallas_call(
        functools.partial(_inproj_kernel, d=d, tn=tn, groups=tuple(groups), rope=rope),
        grid=(t // tm, nj),
        in_specs=in_specs,
        out_specs=pl.BlockSpec((tm, tn), lambda i, j: (i, j)),
        out_shape=jax.ShapeDtypeStruct((t, n), BF16),
        scratch_shapes=[pltpu.VMEM((tm, d), BF16)],
        compiler_params=_cparams(("arbitrary", "arbitrary"), VMEM_LIMIT),
        name="in_proj",
    )(*args)


def _softmax_pv(s, v, extra_logit=None):
    m = jnp.max(s, axis=-1, keepdims=True)
    if extra_logit is not None:
        m = jnp.maximum(m, extra_logit)
    p = jnp.exp(s - m)
    den = jnp.sum(p, axis=-1, keepdims=True)
    if extra_logit is not None:
        den = den + jnp.exp(extra_logit - m)
    o = jnp.dot(p.astype(BF16), v, preferred_element_type=F32)
    return o / den


def _win_attn_kernel(sink_ref, q_ref, kp_ref, kc_ref, kn_ref, vp_ref, vc_ref, vn_ref, kx_ref, vx_ref, o_ref,
                     *, n_grp, nb):
    h = pl.program_id(1)
    n = pl.program_id(2)
    w = WINDOW
    q = q_ref[...]
    qs = jnp.concatenate([q[:, g * HEAD_DIM:(g + 1) * HEAD_DIM] for g in range(n_grp)], axis=0)
    k = jnp.concatenate([kp_ref[...], kc_ref[...], kn_ref[...], kx_ref[...]], axis=0)
    v = jnp.concatenate([vp_ref[...], vc_ref[...], vn_ref[...], vx_ref[...]], axis=0)
    s = lax.dot_general(qs, k, (((1,), (1,)), ((), ())), preferred_element_type=F32) * (HEAD_DIM ** -0.5)
    rows = lax.broadcasted_iota(jnp.int32, s.shape, 0) & (w - 1)
    cols = lax.broadcasted_iota(jnp.int32, s.shape, 1)
    is_x = n < nb
    lo = jnp.where(is_x, jnp.where(n > 0, 0, w), 0)
    hi = jnp.where(is_x, jnp.where(n < nb - 1, 3 * w, 2 * w), 0)
    local_ok = (jnp.abs(cols - w - rows) <= WINDOW) & (cols >= lo) & (cols < hi)
    s = jnp.where(local_ok | (cols >= 3 * w), s, NEG_INF)
    for g in range(n_grp):
        o = _softmax_pv(s[g * w:(g + 1) * w], v, sink_ref[h, g])
        o_ref[:, g * HEAD_DIM:(g + 1) * HEAD_DIM] = o.astype(o_ref.dtype)


def _win_attn(px, sink, *, batch, seq, ctx_len, a_qw, n_kv, k_off, v_off, with_ctx):
    t = px.shape[0]
    w = WINDOW
    n_grp = a_qw // HEAD_DIM // n_kv
    nb = seq // w
    nq = nb + (ctx_len // w if with_ctx else 0)
    qw = n_grp * HEAD_DIM
    ctx_blk0 = batch * seq // ctx_len

    def q_idx(b, h, n):
        return (jnp.where(n < nb, b * nb + n, batch * nb + b * (ctx_len // w) + (n - nb)), h)

    def kv_idx(off, delta):
        def f(b, h, n):
            return (b * nb + jnp.clip(n + delta, 0, nb - 1), off // HEAD_DIM + h)
        return f

    def ctx_idx(off):
        return lambda b, h, n: (ctx_blk0 + b, off // HEAD_DIM + h)

    blk = lambda f: pl.BlockSpec((w, HEAD_DIM), f)
    in_specs = [pl.BlockSpec(memory_space=pltpu.SMEM),
                pl.BlockSpec((w, qw), q_idx),
                blk(kv_idx(k_off, -1)), blk(kv_idx(k_off, 0)), blk(kv_idx(k_off, 1)),
                blk(kv_idx(v_off, -1)), blk(kv_idx(v_off, 0)), blk(kv_idx(v_off, 1)),
                pl.BlockSpec((ctx_len, HEAD_DIM), ctx_idx(k_off)),
                pl.BlockSpec((ctx_len, HEAD_DIM), ctx_idx(v_off))]
    return pl.pallas_call(
        functools.partial(_win_attn_kernel, n_grp=n_grp, nb=nb),
        grid=(batch, n_kv, nq),
        in_specs=in_specs,
        out_specs=pl.BlockSpec((w, qw), q_idx),
        out_shape=jax.ShapeDtypeStruct((t if with_ctx else batch * seq, a_qw), BF16),
        compiler_params=_cparams(("arbitrary",) * 3, VMEM_LIMIT),
        name="win_attn",
    )(sink.reshape(n_kv, n_grp).astype(F32), *([px] * 9))


def _na_attn_kernel(q_ref, kp_ref, kc_ref, kn_ref, vp_ref, vc_ref, vn_ref, kx_ref, vx_ref, bias_ref, o_ref, *, n_rb):
    rb = pl.program_id(2)
    nloc = 3 * NA_RB * GRID_W
    k = jnp.concatenate([kp_ref[...], kc_ref[...], kn_ref[...], kx_ref[...]], axis=0)
    v = jnp.concatenate([vp_ref[...], vc_ref[...], vn_ref[...], vx_ref[...]], axis=0)
    s = lax.dot_general(q_ref[...], k, (((1,), (1,)), ((), ())), preferred_element_type=F32) * (HEAD_DIM ** -0.5)
    s_loc = jnp.where(rb < n_rb, s[:, :nloc] + bias_ref[0], NEG_INF)
    s = jnp.concatenate([s_loc, s[:, nloc:]], axis=1)
    o_ref[...] = _softmax_pv(s, v).astype(o_ref.dtype)


def _na_bias_table(rpb, rows):
    n_heads, n_dr, n_dc = rpb.shape
    kh, kw = (n_dr + 1) // 2, (n_dc + 1) // 2
    n_rb = rows // NA_RB
    cidx = np.arange(GRID_W)
    cs = np.clip(cidx - kw // 2, 0, GRID_W - kw)
    col_ok = (cidx[None, :] >= cs[:, None]) & (cidx[None, :] < cs[:, None] + kw)
    dc_idx = np.clip(cidx[None, :] - cidx[:, None], -(kw - 1), kw - 1) + kw - 1
    a = jnp.where(col_ok[None, None], rpb.astype(F32)[:, :, dc_idx], NEG_INF)
    masked = jnp.full((n_heads, GRID_W, GRID_W), NEG_INF, F32)
    classes = []
    for rb in (0, min(1, n_rb - 1), n_rb - 1):
        qrows = []
        for j in range(NA_RB):
            r = rb * NA_RB + j
            rs = int(np.clip(r - kh // 2, 0, rows - kh))
            blocks = []
            for tblk in range(3):
                for krl in range(NA_RB):
                    kr = (rb - 1 + tblk) * NA_RB + krl
                    ok = (rs <= kr < rs + kh) and (0 <= kr < rows)
                    blocks.append(a[:, kr - r + kh - 1] if ok else masked)
            qrows.append(jnp.concatenate(blocks, axis=-1))
        classes.append(jnp.concatenate(qrows, axis=1))
    tab = jnp.stack(classes, axis=1)
    return tab.reshape(n_heads * 3, NA_RB * GRID_W, 3 * NA_RB * GRID_W)


def _na_attn(px, bias, *, batch, seq, ctx_len, n_heads, with_ctx):
    t = px.shape[0]
    qb = NA_RB * GRID_W
    assert ctx_len == qb, "context queries are processed as one extra query block"
    n_rb = seq // qb
    nq = n_rb + (1 if with_ctx else 0)
    na_w = n_heads * HEAD_DIM
    ctx_blk0 = batch * seq // ctx_len

    def q_idx(b, h, r):
        return (jnp.where(r < n_rb, b * n_rb + r, batch * n_rb + b), h)

    def kv_idx(off, delta):
        return lambda b, h, r: (b * n_rb + jnp.clip(r + delta, 0, n_rb - 1), off // HEAD_DIM + h)

    def ctx_idx(off):
        return lambda b, h, r: (ctx_blk0 + b, off // HEAD_DIM + h)

    def bias_idx(b, h, r):
        return (h * 3 + jnp.where(r == 0, 0, jnp.where(r >= n_rb - 1, 2, 1)), 0, 0)

    blk = lambda f: pl.BlockSpec((qb, HEAD_DIM), f)
    in_specs = [blk(q_idx),
                blk(kv_idx(na_w, -1)), blk(kv_idx(na_w, 0)), blk(kv_idx(na_w, 1)),
                blk(kv_idx(2 * na_w, -1)), blk(kv_idx(2 * na_w, 0)), blk(kv_idx(2 * na_w, 1)),
                pl.BlockSpec((ctx_len, HEAD_DIM), ctx_idx(na_w)),
                pl.BlockSpec((ctx_len, HEAD_DIM), ctx_idx(2 * na_w)),
                pl.BlockSpec((1, qb, 3 * qb), bias_idx)]
    return pl.pallas_call(
        functools.partial(_na_attn_kernel, n_rb=n_rb),
        grid=(batch, n_heads, nq),
        in_specs=in_specs,
        out_specs=blk(q_idx),
        out_shape=jax.ShapeDtypeStruct((t if with_ctx else batch * seq, na_w), BF16),
        compiler_params=_cparams(("arbitrary",) * 3, VMEM_LIMIT),
        name="na_attn",
    )(*([px] * 9), bias)


def _conv_kernel(a_ref, g_ref, ap_ref, gp_ref, an_ref, gn_ref, w_ref, b_ref, lg_ref, lb_ref, o_ref, hbuf, cbuf,
                 *, tiles_per_seq, n_x_tiles, n_taps, tmc, sub):
    i = pl.program_id(0)
    p = i % tiles_per_seq
    is_x = i < n_x_tiles
    has_prev = is_x & (p > 0)
    has_next = is_x & (p < tiles_per_seq - 1)

    def glu(a, g):
        return a.astype(F32) * jax.nn.sigmoid(g.astype(F32))

    hbuf[0:HALO, :] = jnp.where(has_prev, glu(ap_ref[...], gp_ref[...]), 0.0)
    hbuf[HALO:HALO + tmc, :] = glu(a_ref[...], g_ref[...])
    hbuf[HALO + tmc:2 * HALO + tmc, :] = jnp.where(has_next, glu(an_ref[...], gn_ref[...]), 0.0)

    ch = a_ref.shape[1]
    first = HALO - n_taps // 2
    for c in range(ch // LANES):
        cl = slice(c * LANES, (c + 1) * LANES)
        for tb in range(tmc // sub):
            acc = jnp.zeros((sub, LANES), F32)
            for k in range(n_taps):
                r0 = tb * sub + first + k
                acc = acc + hbuf[r0:r0 + sub, cl] * w_ref[k:k + 1, cl]
            cbuf[tb * sub:(tb + 1) * sub, cl] = acc + b_ref[:, cl]

    y = cbuf[...]
    mu = jnp.mean(y, axis=-1, keepdims=True)
    yc = y - mu
    var = jnp.mean(yc * yc, axis=-1, keepdims=True)
    yn = yc * lax.rsqrt(var + EPS) * lg_ref[...] + lb_ref[...]
    o_ref[...] = _silu(yn).astype(o_ref.dtype)


def _conv(px, conv_w, conv_b, ln_g, ln_b, *, a_off, g_off, batch, seq, ctx_len, with_ctx, tmc):
    t = px.shape[0]
    n_taps, ch = conv_w.shape
    assert n_taps // 2 <= HALO and ctx_len == tmc and seq % tmc == 0
    tiles_per_seq = seq // tmc
    n_x_tiles = batch * tiles_per_seq
    n_tiles = n_x_tiles + (batch if with_ctx else 0)
    hpt = tmc // HALO
    n_hblk = t // HALO
    w_pad = jnp.zeros((32, ch), F32).at[:n_taps].set(conv_w.astype(F32))

    main = lambda off: pl.BlockSpec((tmc, ch), lambda i: (i, off // ch))
    prev = lambda off: pl.BlockSpec((HALO, ch), lambda i: (jnp.maximum(i * hpt - 1, 0), off // ch))
    nxt = lambda off: pl.BlockSpec((HALO, ch), lambda i: (jnp.minimum((i + 1) * hpt, n_hblk - 1), off // ch))
    vec = lambda: pl.BlockSpec((1, ch), lambda i: (0, 0))
    return pl.pallas_call(
        functools.partial(_conv_kernel, tiles_per_seq=tiles_per_seq, n_x_tiles=n_x_tiles, n_taps=n_taps,
                          tmc=tmc, sub=64),
        grid=(n_tiles,),
        in_specs=[main(a_off), main(g_off), prev(a_off), prev(g_off), nxt(a_off), nxt(g_off),
                  pl.BlockSpec((32, ch), lambda i: (0, 0)), vec(), vec(), vec()],
        out_specs=pl.BlockSpec((tmc, ch), lambda i: (i, 0)),
        out_shape=jax.ShapeDtypeStruct((t if with_ctx else batch * seq, ch), BF16),
        scratch_shapes=[pltpu.VMEM((tmc + 2 * HALO, ch), F32), pltpu.VMEM((tmc, ch), F32)],
        compiler_params=_cparams(("arbitrary",), VMEM_LIMIT),
        name="conformer_conv",
    )(px, px, px, px, px, px, w_pad, conv_b.reshape(1, ch).astype(F32), ln_g.reshape(1, ch).astype(F32),
      ln_b.reshape(1, ch).astype(F32))


def _outproj_kernel(*refs, d, n_in):
    x_ref, mod_ref, g_ref, rw_ref = refs[:4]
    a_refs = refs[4:4 + n_in]
    w_refs = refs[4 + n_in:4 + 2 * n_in]
    xo_ref, h_ref, lg_ref = refs[4 + 2 * n_in:]
    o = jnp.dot(a_refs[0][...], w_refs[0][...], preferred_element_type=F32)
    for a_ref, w_ref in zip(a_refs[1:], w_refs[1:]):
        o = o + jnp.dot(a_ref[...], w_ref[...], preferred_element_type=F32)
    x_new = x_ref[...] + mod_ref[0, :, 2 * d:3 * d] * o
    xo_ref[...] = x_new
    hf = _rms(x_new, g_ref[...]) * (1.0 + mod_ref[0, :, 4 * d:5 * d]) + mod_ref[0, :, 3 * d:4 * d]
    h_ref[...] = hf
    lg_ref[...] = jnp.dot(hf, rw_ref[...], precision=lax.Precision.HIGHEST, preferred_element_type=F32)


def _outproj(tok, mod3, layer, norm_g, rw_pad, acts, ws, *, t_act, n_x_tiles, tiles_per_seq, tm):
    d = tok.shape[1]
    n_batch = n_x_tiles // tiles_per_seq
    n_in = len(acts)

    def mod_idx(i):
        return (layer * 8 + jnp.where(i < n_x_tiles, i // tiles_per_seq, n_batch), 0, 0)

    in_specs = [pl.BlockSpec((tm, d), lambda i: (i, 0)),
                pl.BlockSpec((1, 1, mod3.shape[2]), mod_idx),
                pl.BlockSpec((1, d), lambda i: (0, 0)),
                pl.BlockSpec(rw_pad.shape, lambda i: (0, 0))]
    in_specs += [pl.BlockSpec((tm, a.shape[1]), lambda i: (i, 0)) for a in acts]
    in_specs += [pl.BlockSpec(w.shape, lambda i: (0, 0)) for w in ws]
    return pl.pallas_call(
        functools.partial(_outproj_kernel, d=d, n_in=n_in),
        grid=(t_act // tm,),
        in_specs=in_specs,
        out_specs=[pl.BlockSpec((tm, d), lambda i: (i, 0)),
                   pl.BlockSpec((tm, d), lambda i: (i, 0)),
                   pl.BlockSpec((tm, LANES), lambda i: (i, 0))],
        out_shape=[jax.ShapeDtypeStruct((t_act, d), F32),
                   jax.ShapeDtypeStruct((t_act, d), F32),
                   jax.ShapeDtypeStruct((t_act, LANES), F32)],
        compiler_params=_cparams(("arbitrary",), VMEM_LIMIT),
        name="out_proj",
    )(tok, mod3, norm_g.reshape(1, d), rw_pad, *acts, *ws)


def _moe_kernel(be_ref, src_ref, srcn_ref, dst_ref, h_hbm, ws_ref, wg_ref, wu_ref, wd_ref, f_hbm,
                xbuf, ybuf, gsem, ssem, *, nb):
    b = pl.program_id(0)
    slot = b % 2
    unroll = 8

    def gather_copy(idx_ref, r, s):
        return pltpu.make_async_copy(h_hbm.at[pl.ds(idx_ref[0, 0, r], 1)], xbuf.at[s, pl.ds(r, 1)], gsem.at[s])

    def scatter_copy(r, s):
        return pltpu.make_async_copy(ybuf.at[s, pl.ds(r, 1)], f_hbm.at[pl.ds(dst_ref[0, 0, r], 1)], ssem.at[s])

    def issue(make):
        def body(it, carry):
            for u in range(unroll):
                make(it * unroll + u).start()
            return carry
        lax.fori_loop(0, MOE_BLK // unroll, body, 0)

    @pl.when(b == 0)
    def _():
        issue(lambda r: gather_copy(src_ref, r, 0))

    @pl.when(b + 1 < nb)
    def _():
        issue(lambda r: gather_copy(srcn_ref, r, 1 - slot))

    pltpu.make_async_copy(h_hbm.at[pl.ds(0, MOE_BLK)], xbuf.at[slot], gsem.at[slot]).wait()

    xb = xbuf[slot].astype(BF16)
    g = jnp.dot(xb, wg_ref[0], preferred_element_type=F32)
    u = jnp.dot(xb, wu_ref[0], preferred_element_type=F32)
    act = (_silu(g) * u).astype(BF16)
    y = jnp.dot(act, wd_ref[0], preferred_element_type=F32) * ws_ref[...]

    def wait_scatter(s):
        pltpu.make_async_copy(ybuf.at[s], f_hbm.at[pl.ds(0, MOE_BLK)], ssem.at[s]).wait()

    @pl.when(b >= 2)
    def _():
        wait_scatter(slot)

    ybuf[slot] = y
    issue(lambda r: scatter_copy(r, slot))

    @pl.when(b == nb - 1)
    def _():
        wait_scatter(slot)
        if nb >= 2:
            wait_scatter(1 - slot)


def _moe(h, be, src, dst, wsort, wg, wu, wd, *, f_rows):
    t, d = h.shape
    nb = be.shape[0]
    n_exp, _, d_exp = wg.shape
    smem_blk = lambda f: pl.BlockSpec((1, 1, MOE_BLK), f, memory_space=pltpu.SMEM)
    grid_spec = pltpu.PrefetchScalarGridSpec(
        num_scalar_prefetch=1,
        grid=(nb,),
        in_specs=[smem_blk(lambda b, be: (b, 0, 0)),
                  smem_blk(lambda b, be: (jnp.minimum(b + 1, nb - 1), 0, 0)),
                  smem_blk(lambda b, be: (b, 0, 0)),
                  pl.BlockSpec(memory_space=pl.ANY),
                  pl.BlockSpec((MOE_BLK, 1), lambda b, be: (b, 0)),
                  pl.BlockSpec((1, d, d_exp), lambda b, be: (be[b], 0, 0)),
                  pl.BlockSpec((1, d, d_exp), lambda b, be: (be[b], 0, 0)),
                  pl.BlockSpec((1, d_exp, d), lambda b, be: (be[b], 0, 0))],
        out_specs=pl.BlockSpec(memory_space=pl.ANY),
        scratch_shapes=[pltpu.VMEM((2, MOE_BLK, d), F32), pltpu.VMEM((2, MOE_BLK, d), F32),
                        pltpu.SemaphoreType.DMA((2,)), pltpu.SemaphoreType.DMA((2,))])
    src3 = src.reshape(nb, 1, MOE_BLK)
    return pl.pallas_call(
        functools.partial(_moe_kernel, nb=nb),
        grid_spec=grid_spec,
        out_shape=jax.ShapeDtypeStruct((f_rows, d), F32),
        compiler_params=_cparams(("arbitrary",), VMEM_LIMIT),
        name="moe_experts",
    )(be, src3, src3, dst.reshape(nb, 1, MOE_BLK), h, wsort.reshape(nb * MOE_BLK, 1), wg, wu, wd)


def _route(logits, router_b, n_exp):
    t = logits.shape[0]
    epg = n_exp // N_GROUPS
    aff = jax.nn.sigmoid(logits[:, :n_exp])
    sel = (aff + router_b.astype(F32)).reshape(t, N_GROUPS, epg)
    group_score = lax.top_k(sel, TOP_K)[0].sum(-1)
    grp = jnp.argmax(group_score, axis=-1)
    sel_in = jnp.take_along_axis(sel, grp[:, None, None], axis=1)[:, 0]
    _, local = lax.top_k(sel_in, TOP_K)
    expert = (grp[:, None] * epg + local).astype(jnp.int32)
    gate = jnp.take_along_axis(aff, expert, axis=1)
    gate = gate / jnp.sum(gate, axis=-1, keepdims=True)

    n_assign = t * TOP_K
    nb = -(-(n_assign + n_exp * (MOE_BLK - 1)) // MOE_BLK)
    e_flat = expert.reshape(-1)
    onehot = (e_flat[:, None] == jnp.arange(n_exp, dtype=jnp.int32)[None, :]).astype(jnp.int32)
    csum = jnp.cumsum(onehot, axis=0)
    rank = jnp.take_along_axis(csum, e_flat[:, None], axis=1)[:, 0] - 1
    counts = csum[-1]
    padded = (counts + MOE_BLK - 1) // MOE_BLK * MOE_BLK
    pends = jnp.cumsum(padded)
    pos = (pends - padded)[e_flat] + rank
    block_expert = jnp.minimum(jnp.searchsorted(pends, jnp.arange(nb) * MOE_BLK, side="right"), n_exp - 1)
    tok_flat = jnp.arange(n_assign, dtype=jnp.int32) // TOP_K
    k_flat = jnp.arange(n_assign, dtype=jnp.int32) % TOP_K
    slots = jnp.arange(nb * MOE_BLK, dtype=jnp.int32)
    dump = TOP_K * t + ((slots // MOE_BLK) % 2) * MOE_BLK + slots % MOE_BLK
    src = jnp.zeros((nb * MOE_BLK,), jnp.int32).at[pos].set(tok_flat)
    dst = dump.at[pos].set(k_flat * t + tok_flat)
    wsort = jnp.zeros((nb * MOE_BLK,), F32).at[pos].set(gate.reshape(-1))
    return block_expert.astype(jnp.int32), src, dst, wsort


def _final_kernel(x_ref, mod_ref, f0_ref, f1_ref, o_ref, *, d):
    o_ref[...] = x_ref[...] + mod_ref[0, :, 5 * d:6 * d] * (f0_ref[...] + f1_ref[...])


def _finalize(x_mid, mod3, layer, f, *, n_x_tiles, tiles_per_seq, tm):
    t, d = x_mid.shape
    n_batch = n_x_tiles // tiles_per_seq
    nt = t // tm

    def mod_idx(i):
        return (layer * 8 + jnp.where(i < n_x_tiles, i // tiles_per_seq, n_batch), 0, 0)

    return pl.pallas_call(
        functools.partial(_final_kernel, d=d),
        grid=(nt,),
        in_specs=[pl.BlockSpec((tm, d), lambda i: (i, 0)),
                  pl.BlockSpec((1, 1, mod3.shape[2]), mod_idx),
                  pl.BlockSpec((tm, d), lambda i: (i, 0)),
                  pl.BlockSpec((tm, d), lambda i: (i + nt, 0))],
        out_specs=pl.BlockSpec((tm, d), lambda i: (i, 0)),
        out_shape=jax.ShapeDtypeStruct((t, d), F32),
        compiler_params=_cparams(("arbitrary",), VMEM_LIMIT),
        name="ffn_residual",
    )(x_mid, mod3, f, f)


def _rope_tables(seq, tm):
    nf = HEAD_DIM // 4
    inv = jnp.power(ROPE_BASE, -jnp.arange(nf, dtype=F32) / nf)
    tt = jnp.arange(seq)
    ar = (tt // GRID_W).astype(F32)[:, None] * inv
    ac = (tt % GRID_W).astype(F32)[:, None] * inv
    ang = jnp.concatenate([ar, ar, ac, ac], axis=-1)
    cos = jnp.concatenate([jnp.cos(ang), jnp.ones((tm, HEAD_DIM), F32)], axis=0)
    sin = jnp.concatenate([jnp.sin(ang), jnp.zeros((tm, HEAD_DIM), F32)], axis=0)
    return cos, sin


def kernel(x, c, ctx, c_ctx, ada_w, ada_b, norm_mix_g, norm_ffn_g, ab_w_in, ab_w_out, ab_q_norm, ab_k_norm, ab_sink, conv_w, conv_b, conv_ln_g, conv_ln_b, na_w_in, na_w_out, na_q_norm, na_k_norm, na_rpb, router_w, router_b, moe_w_gate, moe_w_up, moe_w_down):
    batch, seq, d = x.shape
    ctx_len = ctx.shape[1]
    depth = ada_w.shape[0]
    n_exp = router_w.shape[1]
    b_ch = conv_w.shape[-1]
    a_qw = ab_w_out.shape[1] - b_ch
    a_kvw = (ab_w_in.shape[-1] - a_qw - 2 * b_ch) // 2
    n_kv = 2
    na_w = na_w_out.shape[1]
    assert batch + 1 <= 8 and a_qw % b_ch == 0

    tx, tc = batch * seq, batch * ctx_len
    t = tx + tc
    tm_in = 512 if seq % 512 == 0 and tc % 512 == 0 else 256
    tm_out = 256
    tn = 512 if d >= 2048 else 256

    c8 = jnp.zeros((8, d), F32).at[:batch].set(c).at[batch].set(c_ctx)
    mod3 = _ada_all(c8, ada_w, ada_b).reshape(depth * 8, 1, 6 * d)

    cos, sin = _rope_tables(seq, tm_in)
    rw_pad = jnp.zeros((d, LANES), F32).at[:, :n_exp].set(router_w.astype(F32))
    rows = seq // GRID_W

    k0, v0, u0 = a_qw, a_qw + a_kvw, a_qw + 2 * a_kvw
    k_off, v_off = a_qw + 2 * b_ch, a_qw + 2 * b_ch + a_kvw

    def ab_kind(col):
        return "q" if col < a_qw else ("k" if k_off <= col < v_off else "p")

    def na_kind(col):
        return "q" if col < na_w else ("k" if col < 2 * na_w else "p")

    tok = jnp.concatenate([x.reshape(tx, d), ctx.reshape(tc, d)], axis=0)

    for i in range(depth):
        with_ctx = i < depth - 1
        j = i // 2
        t_act = t if with_ctx else tx
        tiles = dict(n_x_tiles=tx // tm_in, tiles_per_seq=seq // tm_in)
        if i % 2 == 0:
            w = ab_w_in[j]
            w_in = jnp.concatenate([w[:, :k0], w[:, u0:], w[:, k0:u0]], axis=1).astype(BF16)
            px = _inproj(tok, mod3, i, norm_mix_g[i], w_in, ab_q_norm[j], ab_k_norm[j], ab_kind, cos, sin,
                         tm=tm_in, tn=tn, **tiles)
            att = _win_attn(px, ab_sink[j], batch=batch, seq=seq, ctx_len=ctx_len, a_qw=a_qw, n_kv=n_kv,
                            k_off=k_off, v_off=v_off, with_ctx=with_ctx)
            cv = _conv(px, conv_w[j], conv_b[j], conv_ln_g[j], conv_ln_b[j], a_off=a_qw, g_off=a_qw + b_ch,
                       batch=batch, seq=seq, ctx_len=ctx_len, with_ctx=with_ctx, tmc=256)
            w_out = ab_w_out[j].astype(BF16)
            acts, ws = [att, cv], [w_out[:a_qw], w_out[a_qw:]]
        else:
            w_in = na_w_in[j].astype(BF16)
            px = _inproj(tok, mod3, i, norm_mix_g[i], w_in, na_q_norm[j], na_k_norm[j], na_kind, None, None,
                         tm=tm_in, tn=tn, **tiles)
            bias = _na_bias_table(na_rpb[j], rows)
            att = _na_attn(px, bias, batch=batch, seq=seq, ctx_len=ctx_len, n_heads=na_w // HEAD_DIM,
                           with_ctx=with_ctx)
            acts, ws = [att], [na_w_out[j].astype(BF16)]

        otiles = dict(n_x_tiles=tx // tm_out, tiles_per_seq=seq // tm_out)
        x_mid, hf, logits = _outproj(tok, mod3, i, norm_ffn_g[i], rw_pad, acts, ws, t_act=t_act, tm=tm_out, **otiles)
        be, src, dst, wsort = _route(logits, router_b, n_exp)
        f = _moe(hf, be, src, dst, wsort, moe_w_gate[i].astype(BF16), moe_w_up[i].astype(BF16),
                 moe_w_down[i].astype(BF16), f_rows=TOP_K * t_act + 2 * MOE_BLK)
        assert (TOP_K * t_act) % tm_out == 0
        tok = _finalize(x_mid, mod3, i, f, tm=tm_out, **otiles)

    return tok[:tx].reshape(batch, seq, d)
```

```python
import functools

import numpy as np
import jax
import jax.numpy as jnp
from jax import lax
from jax.experimental import pallas as pl
from jax.experimental.pallas import tpu as pltpu

F32 = jnp.float32
BF16 = jnp.bfloat16
I32 = jnp.int32

HEAD_DIM = 128
LANES = 128
GRID_W = 64
WINDOW = 128
N_GROUPS = 4
TOP_K = 2
ROPE_BASE = 10000.0
EPS = 1e-6
NEG_INF = -1e30
MOE_BLK = 256
ROW_TILE = 256
NA_RB = 4
NA_HG = 4
HALO = 16
DMA_UNROLL = 16
VMEM_LIMIT = 56 * 1024 * 1024
HI_MASK = -65536


def _cparams(sem, vmem=None):
    return pltpu.CompilerParams(dimension_semantics=sem, vmem_limit_bytes=vmem)


def _silu(v):
    return v * jax.nn.sigmoid(v)


def _rms(v, g):
    ms = jnp.mean(v * v, axis=-1, keepdims=True)
    return v * lax.rsqrt(ms + EPS) * g


def _pack_pairs(v):
    half = v.shape[1] // 2
    bits = pltpu.bitcast(v.astype(BF16).astype(F32), I32)
    return (bits[:, :half] & HI_MASK) | lax.shift_right_logical(bits[:, half:], 16)


def _unpack_hi(u):
    return pltpu.bitcast(u & HI_MASK, F32)


def _unpack_lo(u):
    return pltpu.bitcast(lax.shift_left(u, 16), F32)


def _ada_kernel(c_ref, w_ref, b_ref, o_ref):
    sc = _silu(c_ref[...])
    o_ref[0] = jnp.dot(sc.astype(BF16), w_ref[0].astype(BF16), preferred_element_type=F32) + b_ref[0]


def _ada_all(c8, ada_w, ada_b):
    depth, d, n = ada_w.shape
    tn = min(n, 1024)
    return pl.pallas_call(
        _ada_kernel,
        grid=(depth, n // tn),
        in_specs=[pl.BlockSpec((8, d), lambda l, j: (0, 0)),
                  pl.BlockSpec((1, d, tn), lambda l, j: (l, 0, j)),
                  pl.BlockSpec((1, 1, tn), lambda l, j: (l, 0, j))],
        out_specs=pl.BlockSpec((1, 8, tn), lambda l, j: (l, 0, j)),
        out_shape=jax.ShapeDtypeStruct((depth, 8, n), F32),
        compiler_params=_cparams(("arbitrary", "arbitrary"), VMEM_LIMIT),
        name="ada_mod",
    )(c8, ada_w, ada_b.reshape(depth, 1, n))


def _rope(y, cos, sin):
    lane = lax.broadcasted_iota(I32, y.shape, 1)
    first = (lane & 32) == 0
    fwd = pltpu.roll(y, 32, 1)
    bwd = pltpu.roll(y, 96, 1)
    return y * cos + jnp.where(first, -bwd, fwd) * sin


def _inproj_kernel(*refs, d, tn, groups, rope):
    if rope:
        x_ref, mod_ref, g_ref, w_ref, qg_ref, kg_ref, cos_ref, sin_ref, o_ref, h_scr = refs
    else:
        x_ref, mod_ref, g_ref, w_ref, qg_ref, kg_ref, o_ref, h_scr = refs
        cos_ref = sin_ref = None
    j = pl.program_id(1)

    @pl.when(j == 0)
    def _():
        y = _rms(x_ref[...], g_ref[...])
        shift = mod_ref[0, :, 0:d]
        scale = mod_ref[0, :, d:2 * d]
        h_scr[...] = (y * (1.0 + scale) + shift).astype(BF16)

    acc = jnp.dot(h_scr[...], w_ref[...], preferred_element_type=F32)

    for kinds, lo, hi in groups:
        @pl.when((j >= lo) & (j <= hi))
        def _(kinds=kinds):
            if all(k == "p" for k in kinds):
                o_ref[...] = acc.astype(o_ref.dtype)
                return
            for s, kind in enumerate(kinds):
                piece = acc[:, s * LANES:(s + 1) * LANES]
                if kind != "p":
                    piece = _rms(piece, (qg_ref if kind == "q" else kg_ref)[...])
                    if rope:
                        piece = _rope(piece, cos_ref[...], sin_ref[...])
                o_ref[:, s * LANES:(s + 1) * LANES] = piece.astype(o_ref.dtype)


def _inproj(tok, mod3, layer, norm_g, w, w_layer, qg, kg, kind_of_col, cos, sin, *, n_x_tiles, tiles_per_seq, tm, tn):
    t, d = tok.shape
    n = w.shape[2]
    nj = n // tn
    per_j = [tuple(kind_of_col(j * tn + s * LANES) for s in range(tn // LANES)) for j in range(nj)]
    groups = []
    for j, kinds in enumerate(per_j):
        if groups and groups[-1][0] == kinds and groups[-1][2] == j - 1:
            groups[-1] = (kinds, groups[-1][1], j)
        else:
            groups.append((kinds, j, j))
    rope = cos is not None
    n_batch = n_x_tiles // tiles_per_seq

    def mod_idx(i, j):
        return (layer * 8 + jnp.where(i < n_x_tiles, i // tiles_per_seq, n_batch), 0, 0)

    def pos_idx(i, j):
        return (jnp.where(i < n_x_tiles, i % tiles_per_seq, tiles_per_seq), 0)

    in_specs = [pl.BlockSpec((tm, d), lambda i, j: (i, 0)),
                pl.BlockSpec((1, 1, mod3.shape[2]), mod_idx),
                pl.BlockSpec((1, d), lambda i, j: (0, 0)),
                pl.BlockSpec((None, d, tn), lambda i, j: (w_layer, 0, j)),
                pl.BlockSpec((1, HEAD_DIM), lambda i, j: (0, 0)),
                pl.BlockSpec((1, HEAD_DIM), lambda i, j: (0, 0))]
    args = [tok, mod3, norm_g.reshape(1, d), w, qg.reshape(1, HEAD_DIM), kg.reshape(1, HEAD_DIM)]
    if rope:
        in_specs += [pl.BlockSpec((tm, HEAD_DIM), pos_idx), pl.BlockSpec((tm, HEAD_DIM), pos_idx)]
        args += [cos, sin]
    return pl.pallas_call(
        functools.partial(_inproj_kernel, d=d, tn=tn, groups=tuple(groups), rope=rope),
        grid=(t // tm, nj),
        in_specs=in_specs,
        out_specs=pl.BlockSpec((tm, tn), lambda i, j: (i, j)),
        out_shape=jax.ShapeDtypeStruct((t, n), BF16),
        scratch_shapes=[pltpu.VMEM((tm, d), BF16)],
        compiler_params=_cparams(("arbitrary", "arbitrary"), VMEM_LIMIT),
        name="in_proj",
    )(*args)


def _softmax_pv(s, v, extra_logit=None):
    m = jnp.max(s, axis=-1, keepdims=True)
    if extra_logit is not None:
        m = jnp.maximum(m, extra_logit)
    p = jnp.exp(s - m)
    den = jnp.sum(p, axis=-1, keepdims=True)
    if extra_logit is not None:
        den = den + jnp.exp(extra_logit - m)
    o = jnp.dot(p.astype(BF16), v, preferred_element_type=F32)
    return o / den


def _win_attn_kernel(sink_ref, q_ref, kp_ref, kc_ref, kn_ref, vp_ref, vc_ref, vn_ref, kx_ref, vx_ref, o_ref,
                     *, n_grp, nb):
    h = pl.program_id(1)
    n = pl.program_id(2)
    w = WINDOW
    q = q_ref[...]
    qs = jnp.concatenate([q[:, g * HEAD_DIM:(g + 1) * HEAD_DIM] for g in range(n_grp)], axis=0)
    k = jnp.concatenate([kp_ref[...], kc_ref[...], kn_ref[...], kx_ref[...]], axis=0)
    v = jnp.concatenate([vp_ref[...], vc_ref[...], vn_ref[...], vx_ref[...]], axis=0)
    s = lax.dot_general(qs, k, (((1,), (1,)), ((), ())), preferred_element_type=F32) * (HEAD_DIM ** -0.5)
    rows = lax.broadcasted_iota(I32, s.shape, 0) & (w - 1)
    cols = lax.broadcasted_iota(I32, s.shape, 1)
    is_x = n < nb
    lo = jnp.where(is_x, jnp.where(n > 0, 0, w), 0)
    hi = jnp.where(is_x, jnp.where(n < nb - 1, 3 * w, 2 * w), 0)
    local_ok = (jnp.abs(cols - w - rows) <= WINDOW) & (cols >= lo) & (cols < hi)
    s = jnp.where(local_ok | (cols >= 3 * w), s, NEG_INF)
    for g in range(n_grp):
        o = _softmax_pv(s[g * w:(g + 1) * w], v, sink_ref[h, g])
        o_ref[:, g * HEAD_DIM:(g + 1) * HEAD_DIM] = o.astype(o_ref.dtype)


def _win_attn(px, sink, *, batch, seq, ctx_len, a_qw, n_kv, k_off, v_off, with_ctx):
    t = px.shape[0]
    w = WINDOW
    n_grp = a_qw // HEAD_DIM // n_kv
    nb = seq // w
    nq = nb + (ctx_len // w if with_ctx else 0)
    qw = n_grp * HEAD_DIM
    ctx_blk0 = batch * seq // ctx_len

    def q_idx(b, h, n):
        return (jnp.where(n < nb, b * nb + n, batch * nb + b * (ctx_len // w) + (n - nb)), h)

    def kv_idx(off, delta):
        def f(b, h, n):
            return (b * nb + jnp.clip(n + delta, 0, nb - 1), off // HEAD_DIM + h)
        return f

    def ctx_idx(off):
        return lambda b, h, n: (ctx_blk0 + b, off // HEAD_DIM + h)

    blk = lambda f: pl.BlockSpec((w, HEAD_DIM), f)
    in_specs = [pl.BlockSpec(memory_space=pltpu.SMEM),
                pl.BlockSpec((w, qw), q_idx),
                blk(kv_idx(k_off, -1)), blk(kv_idx(k_off, 0)), blk(kv_idx(k_off, 1)),
                blk(kv_idx(v_off, -1)), blk(kv_idx(v_off, 0)), blk(kv_idx(v_off, 1)),
                pl.BlockSpec((ctx_len, HEAD_DIM), ctx_idx(k_off)),
                pl.BlockSpec((ctx_len, HEAD_DIM), ctx_idx(v_off))]
    return pl.pallas_call(
        functools.partial(_win_attn_kernel, n_grp=n_grp, nb=nb),
        grid=(batch, n_kv, nq),
        in_specs=in_specs,
        out_specs=pl.BlockSpec((w, qw), q_idx),
        out_shape=jax.ShapeDtypeStruct((t if with_ctx else batch * seq, a_qw), BF16),
        compiler_params=_cparams(("arbitrary",) * 3, VMEM_LIMIT),
        name="win_attn",
    )(sink.reshape(n_kv, n_grp).astype(F32), *([px] * 9))


def _na_attn_kernel(q_ref, kp_ref, kc_ref, kn_ref, vp_ref, vc_ref, vn_ref, kx_ref, vx_ref, bias_ref, o_ref,
                    *, n_rb, n_hg):
    rb = pl.program_id(2)
    nloc = 3 * NA_RB * GRID_W
    for hh in range(n_hg):
        cl = slice(hh * HEAD_DIM, (hh + 1) * HEAD_DIM)
        k = jnp.concatenate([kp_ref[:, cl], kc_ref[:, cl], kn_ref[:, cl], kx_ref[:, cl]], axis=0)
        v = jnp.concatenate([vp_ref[:, cl], vc_ref[:, cl], vn_ref[:, cl], vx_ref[:, cl]], axis=0)
        s = lax.dot_general(q_ref[:, cl], k, (((1,), (1,)), ((), ())), preferred_element_type=F32) * (HEAD_DIM ** -0.5)
        s_loc = jnp.where(rb < n_rb, s[:, :nloc] + bias_ref[hh, 0], NEG_INF)
        s = jnp.concatenate([s_loc, s[:, nloc:]], axis=1)
        o_ref[:, cl] = _softmax_pv(s, v).astype(o_ref.dtype)


def _na_bias_table(rpb, rows):
    n_heads, n_dr, n_dc = rpb.shape
    kh, kw = (n_dr + 1) // 2, (n_dc + 1) // 2
    n_rb = rows // NA_RB
    cidx = np.arange(GRID_W)
    cs = np.clip(cidx - kw // 2, 0, GRID_W - kw)
    col_ok = (cidx[None, :] >= cs[:, None]) & (cidx[None, :] < cs[:, None] + kw)
    dc_idx = np.clip(cidx[None, :] - cidx[:, None], -(kw - 1), kw - 1) + kw - 1
    a = jnp.where(col_ok[None, None], rpb.astype(F32)[:, :, dc_idx], NEG_INF)
    masked = jnp.full((n_heads, GRID_W, GRID_W), NEG_INF, F32)
    classes = []
    for rb in (0, min(1, n_rb - 1), n_rb - 1):
        qrows = []
        for j in range(NA_RB):
            r = rb * NA_RB + j
            rs = int(np.clip(r - kh // 2, 0, rows - kh))
            blocks = []
            for tblk in range(3):
                for krl in range(NA_RB):
                    kr = (rb - 1 + tblk) * NA_RB + krl
                    ok = (rs <= kr < rs + kh) and (0 <= kr < rows)
                    blocks.append(a[:, kr - r + kh - 1] if ok else masked)
            qrows.append(jnp.concatenate(blocks, axis=-1))
        classes.append(jnp.concatenate(qrows, axis=1))
    return jnp.stack(classes, axis=1)


def _na_attn(px, bias, *, batch, seq, ctx_len, n_heads, with_ctx):
    t = px.shape[0]
    qb = NA_RB * GRID_W
    assert ctx_len == qb, "context queries are processed as one extra query block"
    n_rb = seq // qb
    nq = n_rb + (1 if with_ctx else 0)
    na_w = n_heads * HEAD_DIM
    n_hg = min(NA_HG, n_heads)
    gw = n_hg * HEAD_DIM
    assert n_heads % n_hg == 0
    ctx_blk0 = batch * seq // ctx_len

    def q_idx(b, h, r):
        return (jnp.where(r < n_rb, b * n_rb + r, batch * n_rb + b), h)

    def kv_idx(off, delta):
        return lambda b, h, r: (b * n_rb + jnp.clip(r + delta, 0, n_rb - 1), off // gw + h)

    def ctx_idx(off):
        return lambda b, h, r: (ctx_blk0 + b, off // gw + h)

    def bias_idx(b, h, r):
        return (h, jnp.where(r == 0, 0, jnp.where(r >= n_rb - 1, 2, 1)), 0, 0)

    blk = lambda f: pl.BlockSpec((qb, gw), f)
    in_specs = [blk(q_idx),
                blk(kv_idx(na_w, -1)), blk(kv_idx(na_w, 0)), blk(kv_idx(na_w, 1)),
                blk(kv_idx(2 * na_w, -1)), blk(kv_idx(2 * na_w, 0)), blk(kv_idx(2 * na_w, 1)),
                pl.BlockSpec((ctx_len, gw), ctx_idx(na_w)),
                pl.BlockSpec((ctx_len, gw), ctx_idx(2 * na_w)),
                pl.BlockSpec((n_hg, 1, qb, 3 * qb), bias_idx)]
    return pl.pallas_call(
        functools.partial(_na_attn_kernel, n_rb=n_rb, n_hg=n_hg),
        grid=(batch, n_heads // n_hg, nq),
        in_specs=in_specs,
        out_specs=blk(q_idx),
        out_shape=jax.ShapeDtypeStruct((t if with_ctx else batch * seq, na_w), BF16),
        compiler_params=_cparams(("arbitrary",) * 3, VMEM_LIMIT),
        name="na_attn",
    )(*([px] * 9), bias)


def _conv_kernel(a_ref, g_ref, ap_ref, gp_ref, an_ref, gn_ref, w_ref, b_ref, lg_ref, lb_ref, o_ref, hbuf, cbuf,
                 *, tiles_per_seq, n_x_tiles, n_taps, tmc, sub):
    i = pl.program_id(0)
    p = i % tiles_per_seq
    is_x = i < n_x_tiles
    has_prev = is_x & (p > 0)
    has_next = is_x & (p < tiles_per_seq - 1)

    def glu(a, g):
        return a.astype(F32) * jax.nn.sigmoid(g.astype(F32))

    hbuf[0:HALO, :] = jnp.where(has_prev, glu(ap_ref[...], gp_ref[...]), 0.0)
    hbuf[HALO:HALO + tmc, :] = glu(a_ref[...], g_ref[...])
    hbuf[HALO + tmc:2 * HALO + tmc, :] = jnp.where(has_next, glu(an_ref[...], gn_ref[...]), 0.0)

    ch = a_ref.shape[1]
    first = HALO - n_taps // 2
    for c in range(ch // LANES):
        cl = slice(c * LANES, (c + 1) * LANES)
        for tb in range(tmc // sub):
            acc = jnp.zeros((sub, LANES), F32)
            for k in range(n_taps):
                r0 = tb * sub + first + k
                acc = acc + hbuf[r0:r0 + sub, cl] * w_ref[k:k + 1, cl]
            cbuf[tb * sub:(tb + 1) * sub, cl] = acc + b_ref[:, cl]

    y = cbuf[...]
    mu = jnp.mean(y, axis=-1, keepdims=True)
    yc = y - mu
    var = jnp.mean(yc * yc, axis=-1, keepdims=True)
    yn = yc * lax.rsqrt(var + EPS) * lg_ref[...] + lb_ref[...]
    o_ref[...] = _silu(yn).astype(o_ref.dtype)


def _conv(px, conv_w, conv_b, ln_g, ln_b, *, a_off, g_off, batch, seq, ctx_len, with_ctx, tmc):
    t = px.shape[0]
    n_taps, ch = conv_w.shape
    assert n_taps // 2 <= HALO and ctx_len == tmc and seq % tmc == 0
    tiles_per_seq = seq // tmc
    n_x_tiles = batch * tiles_per_seq
    n_tiles = n_x_tiles + (batch if with_ctx else 0)
    hpt = tmc // HALO
    n_hblk = t // HALO
    w_pad = jnp.zeros((32, ch), F32).at[:n_taps].set(conv_w.astype(F32))

    main = lambda off: pl.BlockSpec((tmc, ch), lambda i: (i, off // ch))
    prev = lambda off: pl.BlockSpec((HALO, ch), lambda i: (jnp.maximum(i * hpt - 1, 0), off // ch))
    nxt = lambda off: pl.BlockSpec((HALO, ch), lambda i: (jnp.minimum((i + 1) * hpt, n_hblk - 1), off // ch))
    vec = lambda: pl.BlockSpec((1, ch), lambda i: (0, 0))
    return pl.pallas_call(
        functools.partial(_conv_kernel, tiles_per_seq=tiles_per_seq, n_x_tiles=n_x_tiles, n_taps=n_taps,
                          tmc=tmc, sub=64),
        grid=(n_tiles,),
        in_specs=[main(a_off), main(g_off), prev(a_off), prev(g_off), nxt(a_off), nxt(g_off),
                  pl.BlockSpec((32, ch), lambda i: (0, 0)), vec(), vec(), vec()],
        out_specs=pl.BlockSpec((tmc, ch), lambda i: (i, 0)),
        out_shape=jax.ShapeDtypeStruct((t if with_ctx else batch * seq, ch), BF16),
        scratch_shapes=[pltpu.VMEM((tmc + 2 * HALO, ch), F32), pltpu.VMEM((tmc, ch), F32)],
        compiler_params=_cparams(("arbitrary",), VMEM_LIMIT),
        name="conformer_conv",
    )(px, px, px, px, px, px, w_pad, conv_b.reshape(1, ch).astype(F32), ln_g.reshape(1, ch).astype(F32),
      ln_b.reshape(1, ch).astype(F32))


def _outproj_kernel(*refs, d, n_in, n_exp):
    x_ref, mod_ref, g_ref, rwh_ref, rwl_ref = refs[:5]
    a_refs = refs[5:5 + n_in]
    w_refs = refs[5 + n_in:5 + 2 * n_in]
    xo_ref, hp_ref, lt_ref = refs[5 + 2 * n_in:]
    o = jnp.dot(a_refs[0][...], w_refs[0][...], preferred_element_type=F32)
    for a_ref, w_ref in zip(a_refs[1:], w_refs[1:]):
        o = o + jnp.dot(a_ref[...], w_ref[...], preferred_element_type=F32)
    x_new = x_ref[...] + mod_ref[0, :, 2 * d:3 * d] * o
    xo_ref[...] = x_new
    hf = _rms(x_new, g_ref[...]) * (1.0 + mod_ref[0, :, 4 * d:5 * d]) + mod_ref[0, :, 3 * d:4 * d]
    hp_ref[...] = _pack_pairs(hf)
    h_hi = hf.astype(BF16)
    h_lo = (hf - h_hi.astype(F32)).astype(BF16)
    lg = (jnp.dot(h_hi, rwh_ref[...], preferred_element_type=F32)
          + jnp.dot(h_lo, rwh_ref[...], preferred_element_type=F32)
          + jnp.dot(h_hi, rwl_ref[...], preferred_element_type=F32))
    lt_ref[...] = lg.T[:n_exp]


def _outproj(tok, mod3, layer, norm_g, rw_hi, rw_lo, acts, w, w_layer, *, t_act, n_exp, n_x_tiles, tiles_per_seq, tm):
    d = tok.shape[1]
    n_batch = n_x_tiles // tiles_per_seq
    n_in = len(acts)

    def mod_idx(i):
        return (layer * 8 + jnp.where(i < n_x_tiles, i // tiles_per_seq, n_batch), 0, 0)

    in_specs = [pl.BlockSpec((tm, d), lambda i: (i, 0)),
                pl.BlockSpec((1, 1, mod3.shape[2]), mod_idx),
                pl.BlockSpec((1, d), lambda i: (0, 0)),
                pl.BlockSpec(rw_hi.shape, lambda i: (0, 0)),
                pl.BlockSpec(rw_lo.shape, lambda i: (0, 0))]
    in_specs += [pl.BlockSpec((tm, a.shape[1]), lambda i: (i, 0)) for a in acts]
    kw = acts[0].shape[1]
    assert all(a.shape[1] == kw for a in acts) and w.shape[1] == kw * n_in
    in_specs += [pl.BlockSpec((None, kw, d), lambda i, r=r: (w_layer, r, 0)) for r in range(n_in)]
    return pl.pallas_call(
        functools.partial(_outproj_kernel, d=d, n_in=n_in, n_exp=n_exp),
        grid=(t_act // tm,),
        in_specs=in_specs,
        out_specs=[pl.BlockSpec((tm, d), lambda i: (i, 0)),
                   pl.BlockSpec((tm, d // 2), lambda i: (i, 0)),
                   pl.BlockSpec((n_exp, tm), lambda i: (0, i))],
        out_shape=[jax.ShapeDtypeStruct((t_act, d), F32),
                   jax.ShapeDtypeStruct((t_act, d // 2), I32),
                   jax.ShapeDtypeStruct((n_exp, t_act), F32)],
        compiler_params=_cparams(("arbitrary",), VMEM_LIMIT),
        name="out_proj",
    )(tok, mod3, norm_g.reshape(1, d), rw_hi, rw_lo, *acts, *([w] * n_in))


def _router_kernel(l_ref, b_ref, e_ref, g_ref, *, n_exp):
    epg = n_exp // N_GROUPS
    aff = jax.nn.sigmoid(l_ref[...])
    sel = aff + b_ref[...]
    row = lambda a, i: a[i:i + 1, :]

    scores = []
    for g in range(N_GROUPS):
        best = None
        for i in range(epg):
            for j in range(i + 1, epg):
                pair = row(sel, g * epg + i) + row(sel, g * epg + j)
                best = pair if best is None else jnp.maximum(best, pair)
        scores.append(best)
    grp = jnp.zeros(scores[0].shape, I32)
    top = scores[0]
    for g in range(1, N_GROUPS):
        better = scores[g] > top
        grp = jnp.where(better, g, grp)
        top = jnp.where(better, scores[g], top)

    def pick(a, i):
        out = row(a, i)
        for g in range(1, N_GROUPS):
            out = jnp.where(grp == g, row(a, g * epg + i), out)
        return out

    v = [pick(sel, i) for i in range(epg)]
    a = [pick(aff, i) for i in range(epg)]
    ranks = []
    for i in range(epg):
        r = jnp.zeros(grp.shape, I32)
        for j in range(epg):
            if j != i:
                ahead = (v[j] > v[i]) | ((v[j] == v[i]) & (j < i)) if j < i else (v[j] > v[i])
                r = r + ahead.astype(I32)
        ranks.append(r)
    zero_i, zero_f = jnp.zeros(grp.shape, I32), jnp.zeros(grp.shape, F32)
    experts, affs = [], []
    for k in range(TOP_K):
        e_k, a_k = zero_i, zero_f
        for i in range(epg):
            hit = ranks[i] == k
            e_k = jnp.where(hit, i, e_k)
            a_k = jnp.where(hit, a[i], a_k)
        experts.append(grp * epg + e_k)
        affs.append(a_k)
    den = affs[0] + affs[1]
    e_ref[...] = jnp.concatenate(experts + [zero_i] * (8 - TOP_K), axis=0)
    g_ref[...] = jnp.concatenate([a_k / den for a_k in affs] + [zero_f] * (8 - TOP_K), axis=0)


def _router(logits_t, router_b):
    n_exp, t = logits_t.shape
    tt = 1024 if t % 1024 == 0 else (512 if t % 512 == 0 else 256)
    return pl.pallas_call(
        functools.partial(_router_kernel, n_exp=n_exp),
        grid=(t // tt,),
        in_specs=[pl.BlockSpec((n_exp, tt), lambda i: (0, i)),
                  pl.BlockSpec((n_exp, 1), lambda i: (0, 0))],
        out_specs=[pl.BlockSpec((8, tt), lambda i: (0, i)), pl.BlockSpec((8, tt), lambda i: (0, i))],
        out_shape=[jax.ShapeDtypeStruct((8, t), I32), jax.ShapeDtypeStruct((8, t), F32)],
        compiler_params=_cparams(("arbitrary",), VMEM_LIMIT),
        name="router",
    )(logits_t, router_b.reshape(n_exp, 1).astype(F32))


def _plan(e_out, n_exp):
    t = e_out.shape[1]
    n_assign = t * TOP_K
    nb = -(-(n_assign + n_exp * (MOE_BLK - 1)) // MOE_BLK)
    e_flat = e_out[:TOP_K].T.reshape(-1)
    onehot = (e_flat[:, None] == jnp.arange(n_exp, dtype=I32)[None, :]).astype(I32)
    csum = jnp.cumsum(onehot, axis=0)
    rank = jnp.take_along_axis(csum, e_flat[:, None], axis=1)[:, 0] - 1
    counts = csum[-1]
    padded = (counts + MOE_BLK - 1) // MOE_BLK * MOE_BLK
    pends = jnp.cumsum(padded)
    pos = ((pends - padded)[e_flat] + rank).astype(I32)
    starts = jnp.arange(nb, dtype=I32) * MOE_BLK
    block_expert = jnp.minimum(jnp.sum((pends[None, :] <= starts[:, None]).astype(I32), axis=1), n_exp - 1)
    n_used = (pends[-1] // MOE_BLK).astype(I32).reshape(1)
    nt = t // ROW_TILE
    pos3 = pos.reshape(nt, ROW_TILE, TOP_K).transpose(0, 2, 1).reshape(nt, 1, TOP_K * ROW_TILE)
    return block_expert.astype(I32), n_used, pos3, nb


def _row_dma_loop(n_rows, make_copies):
    def body(it, carry):
        for u in range(DMA_UNROLL):
            for cp in make_copies(it * DMA_UNROLL + u):
                cp.start()
        return carry
    lax.fori_loop(0, n_rows // DMA_UNROLL, body, 0)


def _dispatch_kernel(pos_ref, hp_ref, xs_in_ref, xs_ref, sbuf, sem, *, n_tiles):
    del xs_in_ref
    i = pl.program_id(0)
    slot = i % 2

    def wait_slot(s):
        for _ in range(TOP_K):
            pltpu.make_async_copy(sbuf.at[s], xs_ref.at[pl.ds(0, ROW_TILE)], sem.at[s]).wait()

    @pl.when(i >= 2)
    def _():
        wait_slot(slot)

    sbuf[slot] = hp_ref[...]

    def copies(r):
        return [pltpu.make_async_copy(sbuf.at[slot, pl.ds(r, 1)],
                                      xs_ref.at[pl.ds(pos_ref[0, 0, k * ROW_TILE + r], 1)], sem.at[slot])
                for k in range(TOP_K)]

    _row_dma_loop(ROW_TILE, copies)

    @pl.when(i == n_tiles - 1)
    def _():
        wait_slot(slot)
        if n_tiles >= 2:
            wait_slot(1 - slot)


def _dispatch(hp, pos3, n_slots):
    t, half = hp.shape
    n_tiles = t // ROW_TILE
    xs0 = jnp.zeros((n_slots, half), I32)
    return pl.pallas_call(
        functools.partial(_dispatch_kernel, n_tiles=n_tiles),
        grid=(n_tiles,),
        in_specs=[pl.BlockSpec((1, 1, TOP_K * ROW_TILE), lambda i: (i, 0, 0), memory_space=pltpu.SMEM),
                  pl.BlockSpec((ROW_TILE, half), lambda i: (i, 0)),
                  pl.BlockSpec(memory_space=pl.ANY)],
        out_specs=pl.BlockSpec(memory_space=pl.ANY),
        out_shape=jax.ShapeDtypeStruct((n_slots, half), I32),
        scratch_shapes=[pltpu.VMEM((2, ROW_TILE, half), I32), pltpu.SemaphoreType.DMA((2,))],
        input_output_aliases={2: 0},
        compiler_params=_cparams(("arbitrary",), VMEM_LIMIT),
        name="moe_dispatch",
    )(pos3, hp, xs0)


def _moe_kernel(be_ref, nu_ref, x_ref, wg_ref, wu_ref, wd_ref, y_ref):
    b = pl.program_id(0)
    half = x_ref.shape[1]

    @pl.when(b < nu_ref[0])
    def _():
        u = x_ref[...]
        x_hi = _unpack_hi(u).astype(BF16)
        x_lo = _unpack_lo(u).astype(BF16)

        def proj(w_ref):
            return (jnp.dot(x_hi, w_ref[0, :half, :], preferred_element_type=F32)
                    + jnp.dot(x_lo, w_ref[0, half:, :], preferred_element_type=F32))

        act = (_silu(proj(wg_ref)) * proj(wu_ref)).astype(BF16)
        y_ref[...] = _pack_pairs(jnp.dot(act, wd_ref[0], preferred_element_type=F32))

    @pl.when(b >= nu_ref[0])
    def _():
        y_ref[...] = jnp.zeros(y_ref.shape, y_ref.dtype)


def _moe(xs, be, n_used, wg, wu, wd, layer):
    n_slots, half = xs.shape
    nb = n_slots // MOE_BLK
    _, d, d_exp = wg.shape
    n_exp = wg.shape[0] // 1
    w_idx = lambda b, be, nu: (be[b], 0, 0)
    grid_spec = pltpu.PrefetchScalarGridSpec(
        num_scalar_prefetch=2,
        grid=(nb,),
        in_specs=[pl.BlockSpec((MOE_BLK, half), lambda b, be, nu: (b, 0)),
                  pl.BlockSpec((1, d, d_exp), w_idx),
                  pl.BlockSpec((1, d, d_exp), w_idx),
                  pl.BlockSpec((1, d_exp, d), w_idx)],
        out_specs=pl.BlockSpec((MOE_BLK, half), lambda b, be, nu: (b, 0)))
    del n_exp, layer
    return pl.pallas_call(
        _moe_kernel,
        grid_spec=grid_spec,
        out_shape=jax.ShapeDtypeStruct((n_slots, half), I32),
        compiler_params=_cparams(("arbitrary",), VMEM_LIMIT),
        name="moe_experts",
    )(be, n_used, xs, wg, wu, wd)


def _combine_kernel(pos_ref, posn_ref, x_ref, mod_ref, gt_ref, y_hbm, o_ref, ybuf, sem, *, d, n_tiles):
    i = pl.program_id(0)
    slot = i % 2
    half = d // 2

    def issue(p_ref, s):
        def copies(r):
            return [pltpu.make_async_copy(y_hbm.at[pl.ds(p_ref[0, 0, k * ROW_TILE + r], 1)],
                                          ybuf.at[s, k, pl.ds(r, 1)], sem.at[s])
                    for k in range(TOP_K)]
        _row_dma_loop(ROW_TILE, copies)

    @pl.when(i == 0)
    def _():
        issue(pos_ref, 0)

    @pl.when(i + 1 < n_tiles)
    def _():
        issue(posn_ref, 1 - slot)

    for k in range(TOP_K):
        pltpu.make_async_copy(y_hbm.at[pl.ds(0, ROW_TILE)], ybuf.at[slot, k], sem.at[slot]).wait()

    u0, u1 = ybuf[slot, 0], ybuf[slot, 1]
    w0, w1 = gt_ref[:, 0:1], gt_ref[:, 1:2]
    f_hi = w0 * _unpack_hi(u0) + w1 * _unpack_hi(u1)
    f_lo = w0 * _unpack_lo(u0) + w1 * _unpack_lo(u1)
    o_ref[:, :half] = x_ref[:, :half] + mod_ref[0, :, 5 * d:5 * d + half] * f_hi
    o_ref[:, half:] = x_ref[:, half:] + mod_ref[0, :, 5 * d + half:6 * d] * f_lo


def _combine(x_mid, mod3, layer, gates, pos3, y, *, n_x_tiles, tiles_per_seq):
    t, d = x_mid.shape
    n_batch = n_x_tiles // tiles_per_seq
    n_tiles = t // ROW_TILE
    half = d // 2

    def mod_idx(i):
        return (layer * 8 + jnp.where(i < n_x_tiles, i // tiles_per_seq, n_batch), 0, 0)

    smem_blk = lambda f: pl.BlockSpec((1, 1, TOP_K * ROW_TILE), f, memory_space=pltpu.SMEM)
    return pl.pallas_call(
        functools.partial(_combine_kernel, d=d, n_tiles=n_tiles),
        grid=(n_tiles,),
        in_specs=[smem_blk(lambda i: (i, 0, 0)),
                  smem_blk(lambda i: (jnp.minimum(i + 1, n_tiles - 1), 0, 0)),
                  pl.BlockSpec((ROW_TILE, d), lambda i: (i, 0)),
                  pl.BlockSpec((1, 1, mod3.shape[2]), mod_idx),
                  pl.BlockSpec((ROW_TILE, TOP_K), lambda i: (i, 0)),
                  pl.BlockSpec(memory_space=pl.ANY)],
        out_specs=pl.BlockSpec((ROW_TILE, d), lambda i: (i, 0)),
        out_shape=jax.ShapeDtypeStruct((t, d), F32),
        scratch_shapes=[pltpu.VMEM((2, TOP_K, ROW_TILE, half), I32), pltpu.SemaphoreType.DMA((2,))],
        compiler_params=_cparams(("arbitrary",), VMEM_LIMIT),
        name="moe_combine",
    )(pos3, pos3, x_mid, mod3, gates, y)


def _rope_tables(seq, tm):
    nf = HEAD_DIM // 4
    inv = jnp.power(ROPE_BASE, -jnp.arange(nf, dtype=F32) / nf)
    tt = jnp.arange(seq)
    ar = (tt // GRID_W).astype(F32)[:, None] * inv
    ac = (tt % GRID_W).astype(F32)[:, None] * inv
    ang = jnp.concatenate([ar, ar, ac, ac], axis=-1)
    cos = jnp.concatenate([jnp.cos(ang), jnp.ones((tm, HEAD_DIM), F32)], axis=0)
    sin = jnp.concatenate([jnp.sin(ang), jnp.zeros((tm, HEAD_DIM), F32)], axis=0)
    return cos, sin


def kernel(x, c, ctx, c_ctx, ada_w, ada_b, norm_mix_g, norm_ffn_g, ab_w_in, ab_w_out, ab_q_norm, ab_k_norm, ab_sink, conv_w, conv_b, conv_ln_g, conv_ln_b, na_w_in, na_w_out, na_q_norm, na_k_norm, na_rpb, router_w, router_b, moe_w_gate, moe_w_up, moe_w_down):
    batch, seq, d = x.shape
    ctx_len = ctx.shape[1]
    depth = ada_w.shape[0]
    n_exp = router_w.shape[1]
    b_ch = conv_w.shape[-1]
    a_qw = ab_w_out.shape[1] - b_ch
    a_kvw = (ab_w_in.shape[-1] - a_qw - 2 * b_ch) // 2
    n_kv = 2
    na_w = na_w_out.shape[1]
    assert batch + 1 <= 8 and a_qw == b_ch and n_exp % 8 == 0

    tx, tc = batch * seq, batch * ctx_len
    t = tx + tc
    tm_in = 512 if seq % 512 == 0 and tc % 512 == 0 else 256
    tn = 512 if d >= 2048 else 256

    c8 = jnp.zeros((8, d), F32).at[:batch].set(c).at[batch].set(c_ctx)
    mod3 = _ada_all(c8, ada_w, ada_b).reshape(depth * 8, 1, 6 * d)

    cos, sin = _rope_tables(seq, tm_in)
    rw = jnp.zeros((d, LANES), F32).at[:, :n_exp].set(router_w.astype(F32))
    rw_hi = rw.astype(BF16)
    rw_lo = (rw - rw_hi.astype(F32)).astype(BF16)
    rows = seq // GRID_W

    k0, v0, u0 = a_qw, a_qw + a_kvw, a_qw + 2 * a_kvw
    k_off, v_off = a_qw + 2 * b_ch, a_qw + 2 * b_ch + a_kvw
    ab_w_in_b = jnp.concatenate([ab_w_in[..., :k0], ab_w_in[..., u0:], ab_w_in[..., k0:u0]], axis=-1).astype(BF16)
    ab_w_out_b = ab_w_out.astype(BF16)
    na_w_in_b = na_w_in.astype(BF16)
    na_w_out_b = na_w_out.astype(BF16)
    d_exp = moe_w_gate.shape[-1]
    wg_all = moe_w_gate.astype(BF16).reshape(depth * n_exp, d, d_exp)
    wu_all = moe_w_up.astype(BF16).reshape(depth * n_exp, d, d_exp)
    wd_all = moe_w_down.astype(BF16).reshape(depth * n_exp, d_exp, d)

    def ab_kind(col):
        return "q" if col < a_qw else ("k" if k_off <= col < v_off else "p")

    def na_kind(col):
        return "q" if col < na_w else ("k" if col < 2 * na_w else "p")

    tok = jnp.concatenate([x.reshape(tx, d), ctx.reshape(tc, d)], axis=0)

    for i in range(depth):
        with_ctx = i < depth - 1
        j = i // 2
        t_act = t if with_ctx else tx
        tiles = dict(n_x_tiles=tx // tm_in, tiles_per_seq=seq // tm_in)
        if i % 2 == 0:
            px = _inproj(tok, mod3, i, norm_mix_g[i], ab_w_in_b, j, ab_q_norm[j], ab_k_norm[j], ab_kind, cos, sin,
                         tm=tm_in, tn=tn, **tiles)
            att = _win_attn(px, ab_sink[j], batch=batch, seq=seq, ctx_len=ctx_len, a_qw=a_qw, n_kv=n_kv,
                            k_off=k_off, v_off=v_off, with_ctx=with_ctx)
            cv = _conv(px, conv_w[j], conv_b[j], conv_ln_g[j], conv_ln_b[j], a_off=a_qw, g_off=a_qw + b_ch,
                       batch=batch, seq=seq, ctx_len=ctx_len, with_ctx=with_ctx, tmc=256)
            acts, w_out = [att, cv], ab_w_out_b
        else:
            px = _inproj(tok, mod3, i, norm_mix_g[i], na_w_in_b, j, na_q_norm[j], na_k_norm[j], na_kind, None, None,
                         tm=tm_in, tn=tn, **tiles)
            bias = _na_bias_table(na_rpb[j], rows)
            att = _na_attn(px, bias, batch=batch, seq=seq, ctx_len=ctx_len, n_heads=na_w // HEAD_DIM,
                           with_ctx=with_ctx)
            acts, w_out = [att], na_w_out_b

        otiles = dict(n_x_tiles=tx // ROW_TILE, tiles_per_seq=seq // ROW_TILE)
        x_mid, hp, logits_t = _outproj(tok, mod3, i, norm_ffn_g[i], rw_hi, rw_lo, acts, w_out, j, t_act=t_act,
                                       n_exp=n_exp, tm=ROW_TILE, **otiles)
        e_out, g_out = _router(logits_t, router_b)
        be, n_used, pos3, nb = _plan(e_out, n_exp)
        xs = _dispatch(hp, pos3, nb * MOE_BLK)
        y = _moe(xs, be + i * n_exp, n_used, wg_all, wu_all, wd_all, i)
        tok = _combine(x_mid, mod3, i, g_out[:TOP_K].T, pos3, y, **otiles)

    return tok[:tx].reshape(batch, seq, d)
```

```python
import functools

import numpy as np
import jax
import jax.numpy as jnp
from jax import lax
from jax.experimental import pallas as pl
from jax.experimental.pallas import tpu as pltpu

F32 = jnp.float32
BF16 = jnp.bfloat16
I32 = jnp.int32

HEAD_DIM = 128
LANES = 128
GRID_W = 64
WINDOW = 128
N_GROUPS = 4
TOP_K = 2
ROPE_BASE = 10000.0
EPS = 1e-6
NEG_INF = -1e30
MOE_BLK = 256
ROW_TILE = 256
NA_RB = 4
NA_HG = 4
HALO = 16
DMA_UNROLL = 64
VMEM_LIMIT = 56 * 1024 * 1024
HI_MASK = -65536
LOG2E = 1.4426950408889634
Q_SCALE = HEAD_DIM ** -0.5 * LOG2E


def _cparams(sem, vmem=None):
    return pltpu.CompilerParams(dimension_semantics=sem, vmem_limit_bytes=vmem)


def _silu(v):
    return v * jax.nn.sigmoid(v)


def _rms(v, g):
    ms = jnp.mean(v * v, axis=-1, keepdims=True)
    return v * lax.rsqrt(ms + EPS) * g


def _pack_pairs(v):
    half = v.shape[1] // 2
    bits = pltpu.bitcast(v.astype(BF16).astype(F32), I32)
    return (bits[:, :half] & HI_MASK) | lax.shift_right_logical(bits[:, half:], 16)


def _unpack_hi(u):
    return pltpu.bitcast(u & HI_MASK, F32)


def _unpack_lo(u):
    return pltpu.bitcast(lax.shift_left(u, 16), F32)


def _ada_kernel(c_ref, w_ref, b_ref, o_ref):
    sc = _silu(c_ref[...])
    o_ref[0] = jnp.dot(sc.astype(BF16), w_ref[0].astype(BF16), preferred_element_type=F32) + b_ref[0]


def _ada_all(c8, ada_w, ada_b):
    depth, d, n = ada_w.shape
    tn = min(n, 1024)
    return pl.pallas_call(
        _ada_kernel,
        grid=(depth, n // tn),
        in_specs=[pl.BlockSpec((8, d), lambda l, j: (0, 0)),
                  pl.BlockSpec((1, d, tn), lambda l, j: (l, 0, j)),
                  pl.BlockSpec((1, 1, tn), lambda l, j: (l, 0, j))],
        out_specs=pl.BlockSpec((1, 8, tn), lambda l, j: (l, 0, j)),
        out_shape=jax.ShapeDtypeStruct((depth, 8, n), F32),
        compiler_params=_cparams(("arbitrary", "arbitrary"), VMEM_LIMIT),
        name="ada_mod",
    )(c8, ada_w, ada_b.reshape(depth, 1, n))


def _rope(y, cos, sin):
    lane = lax.broadcasted_iota(I32, y.shape, 1)
    first = (lane & 32) == 0
    fwd = pltpu.roll(y, 32, 1)
    bwd = pltpu.roll(y, 96, 1)
    return y * cos + jnp.where(first, -bwd, fwd) * sin


def _inproj_kernel(*refs, d, tn, groups, rope):
    if rope:
        x_ref, mod_ref, g_ref, w_ref, qg_ref, kg_ref, cos_ref, sin_ref, o_ref, h_scr = refs
    else:
        x_ref, mod_ref, g_ref, w_ref, qg_ref, kg_ref, o_ref, h_scr = refs
        cos_ref = sin_ref = None
    j = pl.program_id(1)

    @pl.when(j == 0)
    def _():
        y = _rms(x_ref[...], g_ref[...])
        shift = mod_ref[0, :, 0:d]
        scale = mod_ref[0, :, d:2 * d]
        h_scr[...] = (y * (1.0 + scale) + shift).astype(BF16)

    acc = jnp.dot(h_scr[...], w_ref[...], preferred_element_type=F32)

    for kinds, lo, hi in groups:
        @pl.when((j >= lo) & (j <= hi))
        def _(kinds=kinds):
            if all(k == "p" for k in kinds):
                o_ref[...] = acc.astype(o_ref.dtype)
                return
            for s, kind in enumerate(kinds):
                piece = acc[:, s * LANES:(s + 1) * LANES]
                if kind != "p":
                    piece = _rms(piece, (qg_ref if kind == "q" else kg_ref)[...])
                    if rope:
                        piece = _rope(piece, cos_ref[...], sin_ref[...])
                    if kind == "q":
                        piece = piece * Q_SCALE
                o_ref[:, s * LANES:(s + 1) * LANES] = piece.astype(o_ref.dtype)


def _inproj(tok, mod3, layer, norm_g, w, w_layer, qg, kg, kind_of_col, cos, sin, *, n_x_tiles, tiles_per_seq, tm, tn):
    t, d = tok.shape
    n = w.shape[2]
    nj = n // tn
    per_j = [tuple(kind_of_col(j * tn + s * LANES) for s in range(tn // LANES)) for j in range(nj)]
    groups = []
    for j, kinds in enumerate(per_j):
        if groups and groups[-1][0] == kinds and groups[-1][2] == j - 1:
            groups[-1] = (kinds, groups[-1][1], j)
        else:
            groups.append((kinds, j, j))
    rope = cos is not None
    n_batch = n_x_tiles // tiles_per_seq

    def mod_idx(i, j):
        return (layer * 8 + jnp.where(i < n_x_tiles, i // tiles_per_seq, n_batch), 0, 0)

    def pos_idx(i, j):
        return (jnp.where(i < n_x_tiles, i % tiles_per_seq, tiles_per_seq), 0)

    in_specs = [pl.BlockSpec((tm, d), lambda i, j: (i, 0)),
                pl.BlockSpec((1, 1, mod3.shape[2]), mod_idx),
                pl.BlockSpec((1, d), lambda i, j: (0, 0)),
                pl.BlockSpec((None, d, tn), lambda i, j: (w_layer, 0, j)),
                pl.BlockSpec((1, HEAD_DIM), lambda i, j: (0, 0)),
                pl.BlockSpec((1, HEAD_DIM), lambda i, j: (0, 0))]
    args = [tok, mod3, norm_g.reshape(1, d), w, qg.reshape(1, HEAD_DIM), kg.reshape(1, HEAD_DIM)]
    if rope:
        in_specs += [pl.BlockSpec((tm, HEAD_DIM), pos_idx), pl.BlockSpec((tm, HEAD_DIM), pos_idx)]
        args += [cos, sin]
    return pl.pallas_call(
        functools.partial(_inproj_kernel, d=d, tn=tn, groups=tuple(groups), rope=rope),
        grid=(t // tm, nj),
        in_specs=in_specs,
        out_specs=pl.BlockSpec((tm, tn), lambda i, j: (i, j)),
        out_shape=jax.ShapeDtypeStruct((t, n), BF16),
        scratch_shapes=[pltpu.VMEM((tm, d), BF16)],
        compiler_params=_cparams(("arbitrary", "arbitrary"), VMEM_LIMIT),
        name="in_proj",
    )(*args)


def _softmax_pv(s, v, extra_logit=None):
    m = jnp.max(s, axis=-1, keepdims=True)
    if extra_logit is not None:
        m = jnp.maximum(m, extra_logit)
    p = jnp.exp2(s - m)
    den = jnp.sum(p, axis=-1, keepdims=True)
    if extra_logit is not None:
        den = den + jnp.exp2(extra_logit - m)
    o = jnp.dot(p.astype(BF16), v, preferred_element_type=F32)
    return o / den


def _win_attn_kernel(sink_ref, q_ref, kp_ref, kc_ref, kn_ref, vp_ref, vc_ref, vn_ref, kx_ref, vx_ref, o_ref,
                     *, n_grp, nb):
    h = pl.program_id(1)
    n = pl.program_id(2)
    w = WINDOW
    q = q_ref[...]
    qs = jnp.concatenate([q[:, g * HEAD_DIM:(g + 1) * HEAD_DIM] for g in range(n_grp)], axis=0)
    k = jnp.concatenate([kp_ref[...], kc_ref[...], kn_ref[...], kx_ref[...]], axis=0)
    v = jnp.concatenate([vp_ref[...], vc_ref[...], vn_ref[...], vx_ref[...]], axis=0)
    s = lax.dot_general(qs, k, (((1,), (1,)), ((), ())), preferred_element_type=F32)
    rows = lax.broadcasted_iota(I32, s.shape, 0) & (w - 1)
    cols = lax.broadcasted_iota(I32, s.shape, 1)
    is_x = n < nb
    lo = jnp.where(is_x, jnp.where(n > 0, 0, w), 0)
    hi = jnp.where(is_x, jnp.where(n < nb - 1, 3 * w, 2 * w), 0)
    local_ok = (jnp.abs(cols - w - rows) <= WINDOW) & (cols >= lo) & (cols < hi)
    s = jnp.where(local_ok | (cols >= 3 * w), s, NEG_INF)
    for g in range(n_grp):
        o = _softmax_pv(s[g * w:(g + 1) * w], v, sink_ref[h, g] * LOG2E)
        o_ref[:, g * HEAD_DIM:(g + 1) * HEAD_DIM] = o.astype(o_ref.dtype)


def _win_attn(px, sink, *, batch, seq, ctx_len, a_qw, n_kv, k_off, v_off, with_ctx):
    t = px.shape[0]
    w = WINDOW
    n_grp = a_qw // HEAD_DIM // n_kv
    nb = seq // w
    nq = nb + (ctx_len // w if with_ctx else 0)
    qw = n_grp * HEAD_DIM
    ctx_blk0 = batch * seq // ctx_len

    def q_idx(b, h, n):
        return (jnp.where(n < nb, b * nb + n, batch * nb + b * (ctx_len // w) + (n - nb)), h)

    def kv_idx(off, delta):
        def f(b, h, n):
            return (b * nb + jnp.clip(n + delta, 0, nb - 1), off // HEAD_DIM + h)
        return f

    def ctx_idx(off):
        return lambda b, h, n: (ctx_blk0 + b, off // HEAD_DIM + h)

    blk = lambda f: pl.BlockSpec((w, HEAD_DIM), f)
    in_specs = [pl.BlockSpec(memory_space=pltpu.SMEM),
                pl.BlockSpec((w, qw), q_idx),
                blk(kv_idx(k_off, -1)), blk(kv_idx(k_off, 0)), blk(kv_idx(k_off, 1)),
                blk(kv_idx(v_off, -1)), blk(kv_idx(v_off, 0)), blk(kv_idx(v_off, 1)),
                pl.BlockSpec((ctx_len, HEAD_DIM), ctx_idx(k_off)),
                pl.BlockSpec((ctx_len, HEAD_DIM), ctx_idx(v_off))]
    return pl.pallas_call(
        functools.partial(_win_attn_kernel, n_grp=n_grp, nb=nb),
        grid=(batch, n_kv, nq),
        in_specs=in_specs,
        out_specs=pl.BlockSpec((w, qw), q_idx),
        out_shape=jax.ShapeDtypeStruct((t if with_ctx else batch * seq, a_qw), BF16),
        compiler_params=_cparams(("arbitrary",) * 3, VMEM_LIMIT),
        name="win_attn",
    )(sink.reshape(n_kv, n_grp).astype(F32), *([px] * 9))


def _na_attn_kernel(q_ref, kp_ref, kc_ref, kn_ref, vp_ref, vc_ref, vn_ref, kx_ref, vx_ref, bias_ref, o_ref,
                    *, n_hg):
    nloc = 3 * NA_RB * GRID_W
    for hh in range(n_hg):
        cl = slice(hh * HEAD_DIM, (hh + 1) * HEAD_DIM)
        k = jnp.concatenate([kp_ref[:, cl], kc_ref[:, cl], kn_ref[:, cl], kx_ref[:, cl]], axis=0)
        v = jnp.concatenate([vp_ref[:, cl], vc_ref[:, cl], vn_ref[:, cl], vx_ref[:, cl]], axis=0)
        s = lax.dot_general(q_ref[:, cl], k, (((1,), (1,)), ((), ())), preferred_element_type=F32)
        s = jnp.concatenate([s[:, :nloc] + bias_ref[hh, 0], s[:, nloc:]], axis=1)
        o_ref[:, cl] = _softmax_pv(s, v).astype(o_ref.dtype)


def _na_bias_table(rpb, rows):
    n_heads, n_dr, n_dc = rpb.shape
    kh, kw = (n_dr + 1) // 2, (n_dc + 1) // 2
    n_rb = rows // NA_RB
    cidx = np.arange(GRID_W)
    cs = np.clip(cidx - kw // 2, 0, GRID_W - kw)
    col_ok = (cidx[None, :] >= cs[:, None]) & (cidx[None, :] < cs[:, None] + kw)
    dc_idx = np.clip(cidx[None, :] - cidx[:, None], -(kw - 1), kw - 1) + kw - 1
    a = jnp.where(col_ok[None, None], rpb.astype(F32)[:, :, dc_idx] * LOG2E, NEG_INF)
    masked = jnp.full((n_heads, GRID_W, GRID_W), NEG_INF, F32)
    classes = []
    for rb in (0, min(1, n_rb - 1), n_rb - 1):
        qrows = []
        for j in range(NA_RB):
            r = rb * NA_RB + j
            rs = int(np.clip(r - kh // 2, 0, rows - kh))
            blocks = []
            for tblk in range(3):
                for krl in range(NA_RB):
                    kr = (rb - 1 + tblk) * NA_RB + krl
                    ok = (rs <= kr < rs + kh) and (0 <= kr < rows)
                    blocks.append(a[:, kr - r + kh - 1] if ok else masked)
            qrows.append(jnp.concatenate(blocks, axis=-1))
        classes.append(jnp.concatenate(qrows, axis=1))
    classes.append(jnp.full_like(classes[0], NEG_INF))
    return jnp.stack(classes, axis=1)


def _na_attn(px, bias, *, batch, seq, ctx_len, n_heads, with_ctx):
    t = px.shape[0]
    qb = NA_RB * GRID_W
    assert ctx_len == qb, "context queries are processed as one extra query block"
    n_rb = seq // qb
    nq = n_rb + (1 if with_ctx else 0)
    na_w = n_heads * HEAD_DIM
    n_hg = min(NA_HG, n_heads)
    gw = n_hg * HEAD_DIM
    assert n_heads % n_hg == 0
    ctx_blk0 = batch * seq // ctx_len

    def q_idx(b, h, r):
        return (jnp.where(r < n_rb, b * n_rb + r, batch * n_rb + b), h)

    def kv_idx(off, delta):
        return lambda b, h, r: (b * n_rb + jnp.clip(r + delta, 0, n_rb - 1), off // gw + h)

    def ctx_idx(off):
        return lambda b, h, r: (ctx_blk0 + b, off // gw + h)

    def bias_idx(b, h, r):
        return (h, jnp.where(r == 0, 0, jnp.where(r < n_rb - 1, 1, jnp.where(r == n_rb - 1, 2, 3))), 0, 0)

    blk = lambda f: pl.BlockSpec((qb, gw), f)
    in_specs = [blk(q_idx),
                blk(kv_idx(na_w, -1)), blk(kv_idx(na_w, 0)), blk(kv_idx(na_w, 1)),
                blk(kv_idx(2 * na_w, -1)), blk(kv_idx(2 * na_w, 0)), blk(kv_idx(2 * na_w, 1)),
                pl.BlockSpec((ctx_len, gw), ctx_idx(na_w)),
                pl.BlockSpec((ctx_len, gw), ctx_idx(2 * na_w)),
                pl.BlockSpec((n_hg, 1, qb, 3 * qb), bias_idx)]
    return pl.pallas_call(
        functools.partial(_na_attn_kernel, n_hg=n_hg),
        grid=(batch, n_heads // n_hg, nq),
        in_specs=in_specs,
        out_specs=blk(q_idx),
        out_shape=jax.ShapeDtypeStruct((t if with_ctx else batch * seq, na_w), BF16),
        compiler_params=_cparams(("arbitrary",) * 3, VMEM_LIMIT),
        name="na_attn",
    )(*([px] * 9), bias)


def _conv_kernel(a_ref, g_ref, ap_ref, gp_ref, an_ref, gn_ref, w_ref, b_ref, lg_ref, lb_ref, o_ref, hbuf, hs, cbuf,
                 *, tiles_per_seq, n_x_tiles, n_taps, tmc, sub):
    i = pl.program_id(0)
    p = i % tiles_per_seq
    is_x = i < n_x_tiles
    has_prev = is_x & (p > 0)
    has_next = is_x & (p < tiles_per_seq - 1)

    def glu(a, g):
        return a.astype(F32) * jax.nn.sigmoid(g.astype(F32))

    hbuf[0:HALO, :] = jnp.where(has_prev, glu(ap_ref[...], gp_ref[...]), 0.0)
    hbuf[HALO:HALO + tmc, :] = glu(a_ref[...], g_ref[...])
    hbuf[HALO + tmc:2 * HALO + tmc, :] = jnp.where(has_next, glu(an_ref[...], gn_ref[...]), 0.0)

    n_buf = tmc + 2 * HALO
    for sh in range(1, 8):
        hs[sh - 1, 0:n_buf - 8, :] = hbuf[sh:sh + n_buf - 8, :]

    ch = a_ref.shape[1]
    first = HALO - n_taps // 2
    for c in range(ch // LANES):
        cl = slice(c * LANES, (c + 1) * LANES)
        for tb in range(tmc // sub):
            acc = jnp.zeros((sub, LANES), F32)
            for k in range(n_taps):
                sh = (first + k) % 8
                r0 = tb * sub + first + k - sh
                slab = hbuf[r0:r0 + sub, cl] if sh == 0 else hs[sh - 1, r0:r0 + sub, cl]
                acc = acc + slab * w_ref[k:k + 1, cl]
            cbuf[tb * sub:(tb + 1) * sub, cl] = acc + b_ref[:, cl]

    y = cbuf[...]
    mu = jnp.mean(y, axis=-1, keepdims=True)
    yc = y - mu
    var = jnp.mean(yc * yc, axis=-1, keepdims=True)
    yn = yc * lax.rsqrt(var + EPS) * lg_ref[...] + lb_ref[...]
    o_ref[...] = _silu(yn).astype(o_ref.dtype)


def _conv(px, conv_w, conv_b, ln_g, ln_b, *, a_off, g_off, batch, seq, ctx_len, with_ctx, tmc):
    t = px.shape[0]
    n_taps, ch = conv_w.shape
    assert n_taps // 2 <= HALO and ctx_len == tmc and seq % tmc == 0
    tiles_per_seq = seq // tmc
    n_x_tiles = batch * tiles_per_seq
    n_tiles = n_x_tiles + (batch if with_ctx else 0)
    hpt = tmc // HALO
    n_hblk = t // HALO
    w_pad = jnp.zeros((32, ch), F32).at[:n_taps].set(conv_w.astype(F32))

    main = lambda off: pl.BlockSpec((tmc, ch), lambda i: (i, off // ch))
    prev = lambda off: pl.BlockSpec((HALO, ch), lambda i: (jnp.maximum(i * hpt - 1, 0), off // ch))
    nxt = lambda off: pl.BlockSpec((HALO, ch), lambda i: (jnp.minimum((i + 1) * hpt, n_hblk - 1), off // ch))
    vec = lambda: pl.BlockSpec((1, ch), lambda i: (0, 0))
    return pl.pallas_call(
        functools.partial(_conv_kernel, tiles_per_seq=tiles_per_seq, n_x_tiles=n_x_tiles, n_taps=n_taps,
                          tmc=tmc, sub=64),
        grid=(n_tiles,),
        in_specs=[main(a_off), main(g_off), prev(a_off), prev(g_off), nxt(a_off), nxt(g_off),
                  pl.BlockSpec((32, ch), lambda i: (0, 0)), vec(), vec(), vec()],
        out_specs=pl.BlockSpec((tmc, ch), lambda i: (i, 0)),
        out_shape=jax.ShapeDtypeStruct((t if with_ctx else batch * seq, ch), BF16),
        scratch_shapes=[pltpu.VMEM((tmc + 2 * HALO, ch), F32), pltpu.VMEM((7, tmc + 2 * HALO, ch), F32),
                        pltpu.VMEM((tmc, ch), F32)],
        compiler_params=_cparams(("arbitrary",), VMEM_LIMIT),
        name="conformer_conv",
    )(px, px, px, px, px, px, w_pad, conv_b.reshape(1, ch).astype(F32), ln_g.reshape(1, ch).astype(F32),
      ln_b.reshape(1, ch).astype(F32))


def _outproj_kernel(*refs, d, n_in, n_exp):
    x_ref, mod_ref, g_ref, rwc_ref, rwh_ref = refs[:5]
    a_refs = refs[5:5 + n_in]
    w_refs = refs[5 + n_in:5 + 2 * n_in]
    xo_ref, hp_ref, lt_ref = refs[5 + 2 * n_in:]
    o = jnp.dot(a_refs[0][...], w_refs[0][...], preferred_element_type=F32)
    for a_ref, w_ref in zip(a_refs[1:], w_refs[1:]):
        o = o + jnp.dot(a_ref[...], w_ref[...], preferred_element_type=F32)
    x_new = x_ref[...] + mod_ref[0, :, 2 * d:3 * d] * o
    xo_ref[...] = x_new
    hf = _rms(x_new, g_ref[...]) * (1.0 + mod_ref[0, :, 4 * d:5 * d]) + mod_ref[0, :, 3 * d:4 * d]
    hp_ref[...] = _pack_pairs(hf)
    h_hi = hf.astype(BF16)
    h_lo = (hf - h_hi.astype(F32)).astype(BF16)
    both = jnp.dot(h_hi, rwc_ref[...], preferred_element_type=F32)
    lg = both[:, :LANES] + both[:, LANES:] + jnp.dot(h_lo, rwh_ref[...], preferred_element_type=F32)
    lt_ref[...] = lg.T[:n_exp]


def _outproj(tok, mod3, layer, norm_g, rw_cat, rw_hi, acts, w, w_layer, *, t_act, n_exp, n_x_tiles, tiles_per_seq, tm):
    d = tok.shape[1]
    n_batch = n_x_tiles // tiles_per_seq
    n_in = len(acts)

    def mod_idx(i):
        return (layer * 8 + jnp.where(i < n_x_tiles, i // tiles_per_seq, n_batch), 0, 0)

    in_specs = [pl.BlockSpec((tm, d), lambda i: (i, 0)),
                pl.BlockSpec((1, 1, mod3.shape[2]), mod_idx),
                pl.BlockSpec((1, d), lambda i: (0, 0)),
                pl.BlockSpec(rw_cat.shape, lambda i: (0, 0)),
                pl.BlockSpec(rw_hi.shape, lambda i: (0, 0))]
    in_specs += [pl.BlockSpec((tm, a.shape[1]), lambda i: (i, 0)) for a in acts]
    kw = acts[0].shape[1]
    assert all(a.shape[1] == kw for a in acts) and w.shape[1] == kw * n_in
    in_specs += [pl.BlockSpec((None, kw, d), lambda i, r=r: (w_layer, r, 0)) for r in range(n_in)]
    return pl.pallas_call(
        functools.partial(_outproj_kernel, d=d, n_in=n_in, n_exp=n_exp),
        grid=(t_act // tm,),
        in_specs=in_specs,
        out_specs=[pl.BlockSpec((tm, d), lambda i: (i, 0)),
                   pl.BlockSpec((tm, d // 2), lambda i: (i, 0)),
                   pl.BlockSpec((n_exp, tm), lambda i: (0, i))],
        out_shape=[jax.ShapeDtypeStruct((t_act, d), F32),
                   jax.ShapeDtypeStruct((t_act, d // 2), I32),
                   jax.ShapeDtypeStruct((n_exp, t_act), F32)],
        compiler_params=_cparams(("arbitrary",), VMEM_LIMIT),
        name="out_proj",
    )(tok, mod3, norm_g.reshape(1, d), rw_cat, rw_hi, *acts, *([w] * n_in))


def _route_tile(logits, bias, n_exp):
    epg = n_exp // N_GROUPS
    aff = jax.nn.sigmoid(logits)
    sel = aff + bias
    row = lambda a, i: a[i:i + 1, :]

    scores = []
    for g in range(N_GROUPS):
        best = None
        for i in range(epg):
            for j in range(i + 1, epg):
                pair = row(sel, g * epg + i) + row(sel, g * epg + j)
                best = pair if best is None else jnp.maximum(best, pair)
        scores.append(best)
    grp = jnp.zeros(scores[0].shape, I32)
    top = scores[0]
    for g in range(1, N_GROUPS):
        better = scores[g] > top
        grp = jnp.where(better, g, grp)
        top = jnp.where(better, scores[g], top)

    def pick(a, i):
        out = row(a, i)
        for g in range(1, N_GROUPS):
            out = jnp.where(grp == g, row(a, g * epg + i), out)
        return out

    v = [pick(sel, i) for i in range(epg)]
    a = [pick(aff, i) for i in range(epg)]
    ranks = []
    for i in range(epg):
        r = jnp.zeros(grp.shape, I32)
        for j in range(epg):
            if j != i:
                ahead = (v[j] > v[i]) | ((v[j] == v[i]) & (j < i)) if j < i else (v[j] > v[i])
                r = r + ahead.astype(I32)
        ranks.append(r)
    zero_i, zero_f = jnp.zeros(grp.shape, I32), jnp.zeros(grp.shape, F32)
    experts, affs = [], []
    for k in range(TOP_K):
        e_k, a_k = zero_i, zero_f
        for i in range(epg):
            hit = ranks[i] == k
            e_k = jnp.where(hit, i, e_k)
            a_k = jnp.where(hit, a[i], a_k)
        experts.append(grp * epg + e_k)
        affs.append(a_k)
    den = affs[0] + affs[1]
    return experts, [a_k / den for a_k in affs]


def _route_plan_kernel(l_ref, b_ref, tri_ref, low_ref, g_ref, pos_ref, be_ref, nu_ref, cnt_scr, carry_scr,
                       *, n_exp, tt, be_offset):
    ph = pl.program_id(0)
    i = pl.program_id(1)
    experts, gates = _route_tile(l_ref[...], b_ref[...], n_exp)
    eid = lax.broadcasted_iota(I32, (n_exp, tt), 0)
    onehot = [(eid == e_k).astype(F32) for e_k in experts]
    both = onehot[0] + onehot[1]
    tile_cnt = jnp.sum(both, axis=1, keepdims=True)

    @pl.when(ph == 0)
    def _():
        @pl.when(i == 0)
        def _():
            cnt_scr[...] = jnp.zeros(cnt_scr.shape, F32)
        cnt_scr[...] += tile_cnt

    @pl.when(ph == 1)
    def _():
        @pl.when(i == 0)
        def _():
            carry_scr[...] = jnp.zeros(carry_scr.shape, F32)
        blocks = jnp.floor((cnt_scr[...] + (MOE_BLK - 1)) * (1.0 / MOE_BLK))
        blocks_b = jnp.broadcast_to(blocks, (n_exp, LANES)).astype(BF16)
        first_blk = jnp.dot(low_ref[...], blocks_b, preferred_element_type=F32)[:, 0:1]
        prefix = jnp.dot(both.astype(BF16), tri_ref[...], preferred_element_type=F32)
        slot = first_blk * MOE_BLK + carry_scr[...] + prefix
        carry_scr[...] += tile_cnt
        pos = [jnp.sum(oh * slot, axis=0, keepdims=True).astype(I32) for oh in onehot]
        for q in range(tt // ROW_TILE):
            for k in range(TOP_K):
                pos_ref[q, :, k * ROW_TILE:(k + 1) * ROW_TILE] = pos[k][:, q * ROW_TILE:(q + 1) * ROW_TILE]
        g_ref[...] = jnp.concatenate(gates + [jnp.zeros_like(gates[0])] * (8 - TOP_K), axis=0)
        last_blk = first_blk + blocks
        bidx = lax.broadcasted_iota(I32, (n_exp, be_ref.shape[1]), 1).astype(F32)
        owner = jnp.sum((last_blk <= bidx).astype(F32), axis=0, keepdims=True)
        be_ref[...] = jnp.minimum(owner, n_exp - 1.0).astype(I32) + be_offset
        nu_ref[...] = jnp.broadcast_to(jnp.sum(blocks, axis=0, keepdims=True), nu_ref.shape).astype(I32)


def _route_plan(logits_t, router_b, be_offset):
    n_exp, t = logits_t.shape
    tt = next(m for m in (1024, 512, 256) if t % m == 0)
    n_tiles = t // tt
    nb = -(-(t * TOP_K + n_exp * (MOE_BLK - 1)) // MOE_BLK)
    nb_pad = -(-nb // LANES) * LANES
    tok_i = np.arange(tt)
    tri = jnp.asarray(tok_i[:, None] < tok_i[None, :], BF16)
    exp_i = np.arange(n_exp)
    low = jnp.asarray(exp_i[None, :] < exp_i[:, None], BF16)
    const = lambda shape: pl.BlockSpec(shape, lambda ph, i: (0,) * len(shape))
    gates, pos3, be, nu = pl.pallas_call(
        functools.partial(_route_plan_kernel, n_exp=n_exp, tt=tt, be_offset=be_offset),
        grid=(2, n_tiles),
        in_specs=[pl.BlockSpec((n_exp, tt), lambda ph, i: (0, i)),
                  const((n_exp, 1)), const((tt, tt)), const((n_exp, n_exp))],
        out_specs=[pl.BlockSpec((8, tt), lambda ph, i: (0, i * ph)),
                   pl.BlockSpec((tt // ROW_TILE, 1, TOP_K * ROW_TILE), lambda ph, i: (i * ph, 0, 0)),
                   const((1, nb_pad)), const((1, LANES))],
        out_shape=[jax.ShapeDtypeStruct((8, t), F32),
                   jax.ShapeDtypeStruct((t // ROW_TILE, 1, TOP_K * ROW_TILE), I32),
                   jax.ShapeDtypeStruct((1, nb_pad), I32),
                   jax.ShapeDtypeStruct((1, LANES), I32)],
        scratch_shapes=[pltpu.VMEM((n_exp, 1), F32), pltpu.VMEM((n_exp, 1), F32)],
        compiler_params=_cparams(("arbitrary", "arbitrary"), VMEM_LIMIT),
        name="route_plan",
    )(logits_t, router_b.reshape(n_exp, 1).astype(F32), tri, low)
    return gates, pos3, be, nu, nb


def _row_dma_loop(n_rows, make_copies):
    def body(it, carry):
        for u in range(DMA_UNROLL):
            for cp in make_copies(it * DMA_UNROLL + u):
                cp.start()
        return carry
    lax.fori_loop(0, n_rows // DMA_UNROLL, body, 0)


def _dispatch_kernel(pos_ref, hp_ref, xs_in_ref, xs_ref, sbuf, sem, *, n_tiles):
    del xs_in_ref
    i = pl.program_id(0)
    slot = i % 2

    def wait_slot(s):
        for _ in range(TOP_K):
            pltpu.make_async_copy(sbuf.at[s], xs_ref.at[pl.ds(0, ROW_TILE)], sem.at[s]).wait()

    @pl.when(i >= 2)
    def _():
        wait_slot(slot)

    sbuf[slot] = hp_ref[...]

    def copies(r):
        return [pltpu.make_async_copy(sbuf.at[slot, pl.ds(r, 1)],
                                      xs_ref.at[pl.ds(pos_ref[0, 0, k * ROW_TILE + r], 1)], sem.at[slot])
                for k in range(TOP_K)]

    _row_dma_loop(ROW_TILE, copies)

    @pl.when(i == n_tiles - 1)
    def _():
        wait_slot(slot)
        if n_tiles >= 2:
            wait_slot(1 - slot)


def _dispatch(hp, pos3, n_slots):
    t, half = hp.shape
    n_tiles = t // ROW_TILE
    xs0 = jnp.zeros((n_slots, half), I32)
    return pl.pallas_call(
        functools.partial(_dispatch_kernel, n_tiles=n_tiles),
        grid=(n_tiles,),
        in_specs=[pl.BlockSpec((1, 1, TOP_K * ROW_TILE), lambda i: (i, 0, 0), memory_space=pltpu.SMEM),
                  pl.BlockSpec((ROW_TILE, half), lambda i: (i, 0)),
                  pl.BlockSpec(memory_space=pl.ANY)],
        out_specs=pl.BlockSpec(memory_space=pl.ANY),
        out_shape=jax.ShapeDtypeStruct((n_slots, half), I32),
        scratch_shapes=[pltpu.VMEM((2, ROW_TILE, half), I32), pltpu.SemaphoreType.DMA((2,))],
        input_output_aliases={2: 0},
        compiler_params=_cparams(("arbitrary",), VMEM_LIMIT),
        name="moe_dispatch",
    )(pos3, hp, xs0)


def _moe_kernel(be_ref, nu_ref, x_ref, wg_ref, wu_ref, wd_ref, y_ref):
    b = pl.program_id(0)
    half = x_ref.shape[1]

    @pl.when(b < nu_ref[0, 0])
    def _():
        u = x_ref[...]
        x_hi = _unpack_hi(u).astype(BF16)
        x_lo = _unpack_lo(u).astype(BF16)

        def proj(w_ref):
            return (jnp.dot(x_hi, w_ref[0, :half, :], preferred_element_type=F32)
                    + jnp.dot(x_lo, w_ref[0, half:, :], preferred_element_type=F32))

        act = (_silu(proj(wg_ref)) * proj(wu_ref)).astype(BF16)
        y_ref[...] = _pack_pairs(jnp.dot(act, wd_ref[0], preferred_element_type=F32))

    @pl.when(b >= nu_ref[0, 0])
    def _():
        y_ref[...] = jnp.zeros(y_ref.shape, y_ref.dtype)


def _moe(xs, be, n_used, wg, wu, wd):
    n_slots, half = xs.shape
    nb = n_slots // MOE_BLK
    _, d, d_exp = wg.shape
    w_idx = lambda b, be, nu: (be[0, b], 0, 0)
    grid_spec = pltpu.PrefetchScalarGridSpec(
        num_scalar_prefetch=2,
        grid=(nb,),
        in_specs=[pl.BlockSpec((MOE_BLK, half), lambda b, be, nu: (b, 0)),
                  pl.BlockSpec((1, d, d_exp), w_idx),
                  pl.BlockSpec((1, d, d_exp), w_idx),
                  pl.BlockSpec((1, d_exp, d), w_idx)],
        out_specs=pl.BlockSpec((MOE_BLK, half), lambda b, be, nu: (b, 0)))
    return pl.pallas_call(
        _moe_kernel,
        grid_spec=grid_spec,
        out_shape=jax.ShapeDtypeStruct((n_slots, half), I32),
        compiler_params=_cparams(("arbitrary",), VMEM_LIMIT),
        name="moe_experts",
    )(be, n_used, xs, wg, wu, wd)


def _combine_kernel(pos_ref, posn_ref, x_ref, mod_ref, gt_ref, y_hbm, o_ref, ybuf, sem, *, d, n_tiles):
    i = pl.program_id(0)
    slot = i % 2
    half = d // 2

    def issue(p_ref, s):
        def copies(r):
            return [pltpu.make_async_copy(y_hbm.at[pl.ds(p_ref[0, 0, k * ROW_TILE + r], 1)],
                                          ybuf.at[s, k, pl.ds(r, 1)], sem.at[s])
                    for k in range(TOP_K)]
        _row_dma_loop(ROW_TILE, copies)

    @pl.when(i == 0)
    def _():
        issue(pos_ref, 0)

    @pl.when(i + 1 < n_tiles)
    def _():
        issue(posn_ref, 1 - slot)

    for k in range(TOP_K):
        pltpu.make_async_copy(y_hbm.at[pl.ds(0, ROW_TILE)], ybuf.at[slot, k], sem.at[slot]).wait()

    u0, u1 = ybuf[slot, 0], ybuf[slot, 1]
    w0, w1 = gt_ref[:, 0:1], gt_ref[:, 1:2]
    f_hi = w0 * _unpack_hi(u0) + w1 * _unpack_hi(u1)
    f_lo = w0 * _unpack_lo(u0) + w1 * _unpack_lo(u1)
    o_ref[:, :half] = x_ref[:, :half] + mod_ref[0, :, 5 * d:5 * d + half] * f_hi
    o_ref[:, half:] = x_ref[:, half:] + mod_ref[0, :, 5 * d + half:6 * d] * f_lo


def _combine(x_mid, mod3, layer, gates, pos3, y, *, n_x_tiles, tiles_per_seq):
    t, d = x_mid.shape
    n_batch = n_x_tiles // tiles_per_seq
    n_tiles = t // ROW_TILE
    half = d // 2

    def mod_idx(i):
        return (layer * 8 + jnp.where(i < n_x_tiles, i // tiles_per_seq, n_batch), 0, 0)

    smem_blk = lambda f: pl.BlockSpec((1, 1, TOP_K * ROW_TILE), f, memory_space=pltpu.SMEM)
    return pl.pallas_call(
        functools.partial(_combine_kernel, d=d, n_tiles=n_tiles),
        grid=(n_tiles,),
        in_specs=[smem_blk(lambda i: (i, 0, 0)),
                  smem_blk(lambda i: (jnp.minimum(i + 1, n_tiles - 1), 0, 0)),
                  pl.BlockSpec((ROW_TILE, d), lambda i: (i, 0)),
                  pl.BlockSpec((1, 1, mod3.shape[2]), mod_idx),
                  pl.BlockSpec((ROW_TILE, TOP_K), lambda i: (i, 0)),
                  pl.BlockSpec(memory_space=pl.ANY)],
        out_specs=pl.BlockSpec((ROW_TILE, d), lambda i: (i, 0)),
        out_shape=jax.ShapeDtypeStruct((t, d), F32),
        scratch_shapes=[pltpu.VMEM((2, TOP_K, ROW_TILE, half), I32), pltpu.SemaphoreType.DMA((2,))],
        compiler_params=_cparams(("arbitrary",), VMEM_LIMIT),
        name="moe_combine",
    )(pos3, pos3, x_mid, mod3, gates, y)


def _rope_tables(seq, tm):
    nf = HEAD_DIM // 4
    inv = jnp.power(ROPE_BASE, -jnp.arange(nf, dtype=F32) / nf)
    tt = jnp.arange(seq)
    ar = (tt // GRID_W).astype(F32)[:, None] * inv
    ac = (tt % GRID_W).astype(F32)[:, None] * inv
    ang = jnp.concatenate([ar, ar, ac, ac], axis=-1)
    cos = jnp.concatenate([jnp.cos(ang), jnp.ones((tm, HEAD_DIM), F32)], axis=0)
    sin = jnp.concatenate([jnp.sin(ang), jnp.zeros((tm, HEAD_DIM), F32)], axis=0)
    return cos, sin


def kernel(x, c, ctx, c_ctx, ada_w, ada_b, norm_mix_g, norm_ffn_g, ab_w_in, ab_w_out, ab_q_norm, ab_k_norm, ab_sink, conv_w, conv_b, conv_ln_g, conv_ln_b, na_w_in, na_w_out, na_q_norm, na_k_norm, na_rpb, router_w, router_b, moe_w_gate, moe_w_up, moe_w_down):
    batch, seq, d = x.shape
    ctx_len = ctx.shape[1]
    depth = ada_w.shape[0]
    n_exp = router_w.shape[1]
    b_ch = conv_w.shape[-1]
    a_qw = ab_w_out.shape[1] - b_ch
    a_kvw = (ab_w_in.shape[-1] - a_qw - 2 * b_ch) // 2
    n_kv = 2
    na_w = na_w_out.shape[1]
    assert batch + 1 <= 8 and a_qw == b_ch and n_exp % 8 == 0

    tx, tc = batch * seq, batch * ctx_len
    t = tx + tc
    tm_in = next(m for m in (1024, 512, 256) if seq % m == 0 and tc % m == 0)
    tn = 512 if d >= 2048 else 256

    c8 = jnp.zeros((8, d), F32).at[:batch].set(c).at[batch].set(c_ctx)
    mod3 = _ada_all(c8, ada_w, ada_b).reshape(depth * 8, 1, 6 * d)

    cos, sin = _rope_tables(seq, tm_in)
    rw = jnp.zeros((d, LANES), F32).at[:, :n_exp].set(router_w.astype(F32))
    rw_hi = rw.astype(BF16)
    rw_cat = jnp.concatenate([rw_hi, (rw - rw_hi.astype(F32)).astype(BF16)], axis=1)
    rows = seq // GRID_W

    k0, v0, u0 = a_qw, a_qw + a_kvw, a_qw + 2 * a_kvw
    k_off, v_off = a_qw + 2 * b_ch, a_qw + 2 * b_ch + a_kvw
    ab_w_in_b = jnp.concatenate([ab_w_in[..., :k0], ab_w_in[..., u0:], ab_w_in[..., k0:u0]], axis=-1).astype(BF16)
    ab_w_out_b = ab_w_out.astype(BF16)
    na_w_in_b = na_w_in.astype(BF16)
    na_w_out_b = na_w_out.astype(BF16)
    d_exp = moe_w_gate.shape[-1]
    wg_all = moe_w_gate.astype(BF16).reshape(depth * n_exp, d, d_exp)
    wu_all = moe_w_up.astype(BF16).reshape(depth * n_exp, d, d_exp)
    wd_all = moe_w_down.astype(BF16).reshape(depth * n_exp, d_exp, d)

    def ab_kind(col):
        return "q" if col < a_qw else ("k" if k_off <= col < v_off else "p")

    def na_kind(col):
        return "q" if col < na_w else ("k" if col < 2 * na_w else "p")

    tok = jnp.concatenate([x.reshape(tx, d), ctx.reshape(tc, d)], axis=0)

    for i in range(depth):
        with_ctx = i < depth - 1
        j = i // 2
        t_act = t if with_ctx else tx
        tiles = dict(n_x_tiles=tx // tm_in, tiles_per_seq=seq // tm_in)
        if i % 2 == 0:
            px = _inproj(tok, mod3, i, norm_mix_g[i], ab_w_in_b, j, ab_q_norm[j], ab_k_norm[j], ab_kind, cos, sin,
                         tm=tm_in, tn=tn, **tiles)
            att = _win_attn(px, ab_sink[j], batch=batch, seq=seq, ctx_len=ctx_len, a_qw=a_qw, n_kv=n_kv,
                            k_off=k_off, v_off=v_off, with_ctx=with_ctx)
            cv = _conv(px, conv_w[j], conv_b[j], conv_ln_g[j], conv_ln_b[j], a_off=a_qw, g_off=a_qw + b_ch,
                       batch=batch, seq=seq, ctx_len=ctx_len, with_ctx=with_ctx, tmc=256)
            acts, w_out = [att, cv], ab_w_out_b
        else:
            px = _inproj(tok, mod3, i, norm_mix_g[i], na_w_in_b, j, na_q_norm[j], na_k_norm[j], na_kind, None, None,
                         tm=tm_in, tn=tn, **tiles)
            bias = _na_bias_table(na_rpb[j], rows)
            att = _na_attn(px, bias, batch=batch, seq=seq, ctx_len=ctx_len, n_heads=na_w // HEAD_DIM,
                           with_ctx=with_ctx)
            acts, w_out = [att], na_w_out_b

        otiles = dict(n_x_tiles=tx // ROW_TILE, tiles_per_seq=seq // ROW_TILE)
        x_mid, hp, logits_t = _outproj(tok, mod3, i, norm_ffn_g[i], rw_cat, rw_hi, acts, w_out, j, t_act=t_act,
                                       n_exp=n_exp, tm=ROW_TILE, **otiles)
        g_out, pos3, be, n_used, nb = _route_plan(logits_t, router_b, i * n_exp)
        xs = _dispatch(hp, pos3, nb * MOE_BLK)
        y = _moe(xs, be, n_used, wg_all, wu_all, wd_all)
        tok = _combine(x_mid, mod3, i, g_out[:TOP_K].T, pos3, y, **otiles)

    return tok[:tx].reshape(batch, seq, d)
```

```python
import functools

import numpy as np
import jax
import jax.numpy as jnp
from jax import lax
from jax.experimental import pallas as pl
from jax.experimental.pallas import tpu as pltpu

F32 = jnp.float32
BF16 = jnp.bfloat16
I32 = jnp.int32

HEAD_DIM = 128
LANES = 128
GRID_W = 64
WINDOW = 128
N_GROUPS = 4
TOP_K = 2
ROPE_BASE = 10000.0
EPS = 1e-6
NEG_INF = -1e30
MOE_BLK = 256
ROW_TILE = 256
NA_RB = 4
NA_HG = 4
HALO = 16
DMA_UNROLL = 64
VMEM_LIMIT = 56 * 1024 * 1024
MOE_VMEM_LIMIT = 60 * 1024 * 1024
HI_MASK = -65536
LOG2E = 1.4426950408889634
Q_SCALE = HEAD_DIM ** -0.5 * LOG2E


def _cparams(sem, vmem=None):
    return pltpu.CompilerParams(dimension_semantics=sem, vmem_limit_bytes=vmem)


def _silu(v):
    return v * jax.nn.sigmoid(v)


def _rms(v, g):
    ms = jnp.mean(v * v, axis=-1, keepdims=True)
    return v * lax.rsqrt(ms + EPS) * g


def _pack_pairs(v):
    half = v.shape[1] // 2
    bits = pltpu.bitcast(v.astype(BF16).astype(F32), I32)
    return (bits[:, :half] & HI_MASK) | lax.shift_right_logical(bits[:, half:], 16)


def _unpack_hi(u):
    return pltpu.bitcast(u & HI_MASK, F32)


def _unpack_lo(u):
    return pltpu.bitcast(lax.shift_left(u, 16), F32)


def _ada_kernel(c_ref, w_ref, b_ref, o_ref):
    sc = _silu(c_ref[...])
    o_ref[0] = jnp.dot(sc.astype(BF16), w_ref[0].astype(BF16), preferred_element_type=F32) + b_ref[0]


def _ada_all(c8, ada_w, ada_b):
    depth, d, n = ada_w.shape
    tn = min(n, 1024)
    return pl.pallas_call(
        _ada_kernel,
        grid=(depth, n // tn),
        in_specs=[pl.BlockSpec((8, d), lambda l, j: (0, 0)),
                  pl.BlockSpec((1, d, tn), lambda l, j: (l, 0, j)),
                  pl.BlockSpec((1, 1, tn), lambda l, j: (l, 0, j))],
        out_specs=pl.BlockSpec((1, 8, tn), lambda l, j: (l, 0, j)),
        out_shape=jax.ShapeDtypeStruct((depth, 8, n), F32),
        compiler_params=_cparams(("arbitrary", "arbitrary"), VMEM_LIMIT),
        name="ada_mod",
    )(c8, ada_w, ada_b.reshape(depth, 1, n))


def _rope(y, cos, sin):
    lane = lax.broadcasted_iota(I32, y.shape, 1)
    first = (lane & 32) == 0
    fwd = pltpu.roll(y, 32, 1)
    bwd = pltpu.roll(y, 96, 1)
    return y * cos + jnp.where(first, -bwd, fwd) * sin


def _inproj_kernel(*refs, d, tn, groups, rope):
    if rope:
        x_ref, mod_ref, g_ref, w_ref, qg_ref, kg_ref, cos_ref, sin_ref, o_ref, h_scr = refs
    else:
        x_ref, mod_ref, g_ref, w_ref, qg_ref, kg_ref, o_ref, h_scr = refs
        cos_ref = sin_ref = None
    j = pl.program_id(1)

    @pl.when(j == 0)
    def _():
        y = _rms(x_ref[...], g_ref[...])
        shift = mod_ref[0, :, 0:d]
        scale = mod_ref[0, :, d:2 * d]
        h_scr[...] = (y * (1.0 + scale) + shift).astype(BF16)

    acc = jnp.dot(h_scr[...], w_ref[...], preferred_element_type=F32)

    for kinds, lo, hi in groups:
        @pl.when((j >= lo) & (j <= hi))
        def _(kinds=kinds):
            if all(k == "p" for k in kinds):
                o_ref[...] = acc.astype(o_ref.dtype)
                return
            for s, kind in enumerate(kinds):
                piece = acc[:, s * LANES:(s + 1) * LANES]
                if kind != "p":
                    piece = _rms(piece, (qg_ref if kind == "q" else kg_ref)[...])
                    if rope:
                        piece = _rope(piece, cos_ref[...], sin_ref[...])
                    if kind == "q":
                        piece = piece * Q_SCALE
                o_ref[:, s * LANES:(s + 1) * LANES] = piece.astype(o_ref.dtype)


def _inproj(tok, mod3, layer, norm_g, w, w_layer, qg, kg, kind_of_col, cos, sin, *, n_x_tiles, tiles_per_seq, tm, tn):
    t, d = tok.shape
    n = w.shape[2]
    nj = n // tn
    per_j = [tuple(kind_of_col(j * tn + s * LANES) for s in range(tn // LANES)) for j in range(nj)]
    groups = []
    for j, kinds in enumerate(per_j):
        if groups and groups[-1][0] == kinds and groups[-1][2] == j - 1:
            groups[-1] = (kinds, groups[-1][1], j)
        else:
            groups.append((kinds, j, j))
    rope = cos is not None
    n_batch = n_x_tiles // tiles_per_seq

    def mod_idx(i, j):
        return (layer * 8 + jnp.where(i < n_x_tiles, i // tiles_per_seq, n_batch), 0, 0)

    def pos_idx(i, j):
        return (jnp.where(i < n_x_tiles, i % tiles_per_seq, tiles_per_seq), 0)

    in_specs = [pl.BlockSpec((tm, d), lambda i, j: (i, 0)),
                pl.BlockSpec((1, 1, mod3.shape[2]), mod_idx),
                pl.BlockSpec((1, d), lambda i, j: (0, 0)),
                pl.BlockSpec((None, d, tn), lambda i, j: (w_layer, 0, j)),
                pl.BlockSpec((1, HEAD_DIM), lambda i, j: (0, 0)),
                pl.BlockSpec((1, HEAD_DIM), lambda i, j: (0, 0))]
    args = [tok, mod3, norm_g.reshape(1, d), w, qg.reshape(1, HEAD_DIM), kg.reshape(1, HEAD_DIM)]
    if rope:
        in_specs += [pl.BlockSpec((tm, HEAD_DIM), pos_idx), pl.BlockSpec((tm, HEAD_DIM), pos_idx)]
        args += [cos, sin]
    return pl.pallas_call(
        functools.partial(_inproj_kernel, d=d, tn=tn, groups=tuple(groups), rope=rope),
        grid=(t // tm, nj),
        in_specs=in_specs,
        out_specs=pl.BlockSpec((tm, tn), lambda i, j: (i, j)),
        out_shape=jax.ShapeDtypeStruct((t, n), BF16),
        scratch_shapes=[pltpu.VMEM((tm, d), BF16)],
        compiler_params=_cparams(("arbitrary", "arbitrary"), VMEM_LIMIT),
        name="in_proj",
    )(*args)


def _softmax_pv(s, v, extra_logit=None):
    m = jnp.max(s, axis=-1, keepdims=True)
    if extra_logit is not None:
        m = jnp.maximum(m, extra_logit)
    p = jnp.exp2(s - m)
    den = jnp.sum(p, axis=-1, keepdims=True)
    if extra_logit is not None:
        den = den + jnp.exp2(extra_logit - m)
    o = jnp.dot(p.astype(BF16), v, preferred_element_type=F32)
    return o / den


def _win_attn_kernel(sink_ref, q_ref, kp_ref, kc_ref, kn_ref, vp_ref, vc_ref, vn_ref, kx_ref, vx_ref, o_ref,
                     *, n_grp, nb):
    h = pl.program_id(1)
    n = pl.program_id(2)
    w = WINDOW
    q = q_ref[...]
    qs = jnp.concatenate([q[:, g * HEAD_DIM:(g + 1) * HEAD_DIM] for g in range(n_grp)], axis=0)
    k = jnp.concatenate([kp_ref[...], kc_ref[...], kn_ref[...], kx_ref[...]], axis=0)
    v = jnp.concatenate([vp_ref[...], vc_ref[...], vn_ref[...], vx_ref[...]], axis=0)
    s = lax.dot_general(qs, k, (((1,), (1,)), ((), ())), preferred_element_type=F32)
    rows = lax.broadcasted_iota(I32, s.shape, 0) & (w - 1)
    cols = lax.broadcasted_iota(I32, s.shape, 1)
    is_x = n < nb
    lo = jnp.where(is_x, jnp.where(n > 0, 0, w), 0)
    hi = jnp.where(is_x, jnp.where(n < nb - 1, 3 * w, 2 * w), 0)
    local_ok = (jnp.abs(cols - w - rows) <= WINDOW) & (cols >= lo) & (cols < hi)
    s = jnp.where(local_ok | (cols >= 3 * w), s, NEG_INF)
    for g in range(n_grp):
        o = _softmax_pv(s[g * w:(g + 1) * w], v, sink_ref[h, g] * LOG2E)
        o_ref[:, g * HEAD_DIM:(g + 1) * HEAD_DIM] = o.astype(o_ref.dtype)


def _win_attn(px, sink, *, batch, seq, ctx_len, a_qw, n_kv, k_off, v_off, with_ctx):
    t = px.shape[0]
    w = WINDOW
    n_grp = a_qw // HEAD_DIM // n_kv
    nb = seq // w
    nq = nb + (ctx_len // w if with_ctx else 0)
    qw = n_grp * HEAD_DIM
    ctx_blk0 = batch * seq // ctx_len

    def q_idx(b, h, n):
        return (jnp.where(n < nb, b * nb + n, batch * nb + b * (ctx_len // w) + (n - nb)), h)

    def kv_idx(off, delta):
        def f(b, h, n):
            return (b * nb + jnp.clip(n + delta, 0, nb - 1), off // HEAD_DIM + h)
        return f

    def ctx_idx(off):
        return lambda b, h, n: (ctx_blk0 + b, off // HEAD_DIM + h)

    blk = lambda f: pl.BlockSpec((w, HEAD_DIM), f)
    in_specs = [pl.BlockSpec(memory_space=pltpu.SMEM),
                pl.BlockSpec((w, qw), q_idx),
                blk(kv_idx(k_off, -1)), blk(kv_idx(k_off, 0)), blk(kv_idx(k_off, 1)),
                blk(kv_idx(v_off, -1)), blk(kv_idx(v_off, 0)), blk(kv_idx(v_off, 1)),
                pl.BlockSpec((ctx_len, HEAD_DIM), ctx_idx(k_off)),
                pl.BlockSpec((ctx_len, HEAD_DIM), ctx_idx(v_off))]
    return pl.pallas_call(
        functools.partial(_win_attn_kernel, n_grp=n_grp, nb=nb),
        grid=(batch, n_kv, nq),
        in_specs=in_specs,
        out_specs=pl.BlockSpec((w, qw), q_idx),
        out_shape=jax.ShapeDtypeStruct((t if with_ctx else batch * seq, a_qw), BF16),
        compiler_params=_cparams(("arbitrary",) * 3, VMEM_LIMIT),
        name="win_attn",
    )(sink.reshape(n_kv, n_grp).astype(F32), *([px] * 9))


def _na_attn_kernel(q_ref, kp_ref, kc_ref, kn_ref, vp_ref, vc_ref, vn_ref, kx_ref, vx_ref, bias_ref, o_ref,
                    *, n_hg):
    nloc = 3 * NA_RB * GRID_W
    for hh in range(n_hg):
        cl = slice(hh * HEAD_DIM, (hh + 1) * HEAD_DIM)
        k = jnp.concatenate([kp_ref[:, cl], kc_ref[:, cl], kn_ref[:, cl], kx_ref[:, cl]], axis=0)
        v = jnp.concatenate([vp_ref[:, cl], vc_ref[:, cl], vn_ref[:, cl], vx_ref[:, cl]], axis=0)
        s = lax.dot_general(q_ref[:, cl], k, (((1,), (1,)), ((), ())), preferred_element_type=F32)
        s = jnp.concatenate([s[:, :nloc] + bias_ref[hh, 0], s[:, nloc:]], axis=1)
        o_ref[:, cl] = _softmax_pv(s, v).astype(o_ref.dtype)


def _na_bias_table(rpb, rows):
    n_heads, n_dr, n_dc = rpb.shape
    kh, kw = (n_dr + 1) // 2, (n_dc + 1) // 2
    n_rb = rows // NA_RB
    cidx = np.arange(GRID_W)
    cs = np.clip(cidx - kw // 2, 0, GRID_W - kw)
    col_ok = (cidx[None, :] >= cs[:, None]) & (cidx[None, :] < cs[:, None] + kw)
    dc_idx = np.clip(cidx[None, :] - cidx[:, None], -(kw - 1), kw - 1) + kw - 1
    a = jnp.where(col_ok[None, None], rpb.astype(F32)[:, :, dc_idx] * LOG2E, NEG_INF)
    masked = jnp.full((n_heads, GRID_W, GRID_W), NEG_INF, F32)
    classes = []
    for rb in (0, min(1, n_rb - 1), n_rb - 1):
        qrows = []
        for j in range(NA_RB):
            r = rb * NA_RB + j
            rs = int(np.clip(r - kh // 2, 0, rows - kh))
            blocks = []
            for tblk in range(3):
                for krl in range(NA_RB):
                    kr = (rb - 1 + tblk) * NA_RB + krl
                    ok = (rs <= kr < rs + kh) and (0 <= kr < rows)
                    blocks.append(a[:, kr - r + kh - 1] if ok else masked)
            qrows.append(jnp.concatenate(blocks, axis=-1))
        classes.append(jnp.concatenate(qrows, axis=1))
    classes.append(jnp.full_like(classes[0], NEG_INF))
    return jnp.stack(classes, axis=1)


def _na_attn(px, bias, *, batch, seq, ctx_len, n_heads, with_ctx):
    t = px.shape[0]
    qb = NA_RB * GRID_W
    assert ctx_len == qb, "context queries are processed as one extra query block"
    n_rb = seq // qb
    nq = n_rb + (1 if with_ctx else 0)
    na_w = n_heads * HEAD_DIM
    n_hg = min(NA_HG, n_heads)
    gw = n_hg * HEAD_DIM
    assert n_heads % n_hg == 0
    ctx_blk0 = batch * seq // ctx_len

    def q_idx(b, h, r):
        return (jnp.where(r < n_rb, b * n_rb + r, batch * n_rb + b), h)

    def kv_idx(off, delta):
        return lambda b, h, r: (b * n_rb + jnp.clip(r + delta, 0, n_rb - 1), off // gw + h)

    def ctx_idx(off):
        return lambda b, h, r: (ctx_blk0 + b, off // gw + h)

    def bias_idx(b, h, r):
        return (h, jnp.where(r == 0, 0, jnp.where(r < n_rb - 1, 1, jnp.where(r == n_rb - 1, 2, 3))), 0, 0)

    blk = lambda f: pl.BlockSpec((qb, gw), f)
    in_specs = [blk(q_idx),
                blk(kv_idx(na_w, -1)), blk(kv_idx(na_w, 0)), blk(kv_idx(na_w, 1)),
                blk(kv_idx(2 * na_w, -1)), blk(kv_idx(2 * na_w, 0)), blk(kv_idx(2 * na_w, 1)),
                pl.BlockSpec((ctx_len, gw), ctx_idx(na_w)),
                pl.BlockSpec((ctx_len, gw), ctx_idx(2 * na_w)),
                pl.BlockSpec((n_hg, 1, qb, 3 * qb), bias_idx)]
    return pl.pallas_call(
        functools.partial(_na_attn_kernel, n_hg=n_hg),
        grid=(batch, n_heads // n_hg, nq),
        in_specs=in_specs,
        out_specs=blk(q_idx),
        out_shape=jax.ShapeDtypeStruct((t if with_ctx else batch * seq, na_w), BF16),
        compiler_params=_cparams(("arbitrary",) * 3, VMEM_LIMIT),
        name="na_attn",
    )(*([px] * 9), bias)


def _conv_kernel(a_ref, g_ref, ap_ref, gp_ref, an_ref, gn_ref, w_ref, b_ref, lg_ref, lb_ref, o_ref, hbuf, hs, cbuf,
                 *, tiles_per_seq, n_x_tiles, n_taps, tmc, sub):
    i = pl.program_id(0)
    p = i % tiles_per_seq
    is_x = i < n_x_tiles
    has_prev = is_x & (p > 0)
    has_next = is_x & (p < tiles_per_seq - 1)

    def glu(a, g):
        return a.astype(F32) * jax.nn.sigmoid(g.astype(F32))

    hbuf[0:HALO, :] = jnp.where(has_prev, glu(ap_ref[...], gp_ref[...]), 0.0)
    hbuf[HALO:HALO + tmc, :] = glu(a_ref[...], g_ref[...])
    hbuf[HALO + tmc:2 * HALO + tmc, :] = jnp.where(has_next, glu(an_ref[...], gn_ref[...]), 0.0)

    n_buf = tmc + 2 * HALO
    for sh in range(1, 8):
        hs[sh - 1, 0:n_buf - 8, :] = hbuf[sh:sh + n_buf - 8, :]

    ch = a_ref.shape[1]
    first = HALO - n_taps // 2
    for c in range(ch // LANES):
        cl = slice(c * LANES, (c + 1) * LANES)
        for tb in range(tmc // sub):
            acc = jnp.zeros((sub, LANES), F32)
            for k in range(n_taps):
                sh = (first + k) % 8
                r0 = tb * sub + first + k - sh
                slab = hbuf[r0:r0 + sub, cl] if sh == 0 else hs[sh - 1, r0:r0 + sub, cl]
                acc = acc + slab * w_ref[k:k + 1, cl]
            cbuf[tb * sub:(tb + 1) * sub, cl] = acc + b_ref[:, cl]

    y = cbuf[...]
    mu = jnp.mean(y, axis=-1, keepdims=True)
    yc = y - mu
    var = jnp.mean(yc * yc, axis=-1, keepdims=True)
    yn = yc * lax.rsqrt(var + EPS) * lg_ref[...] + lb_ref[...]
    o_ref[...] = _silu(yn).astype(o_ref.dtype)


def _conv(px, conv_w, conv_b, ln_g, ln_b, *, a_off, g_off, batch, seq, ctx_len, with_ctx, tmc):
    t = px.shape[0]
    n_taps, ch = conv_w.shape
    assert n_taps // 2 <= HALO and ctx_len == tmc and seq % tmc == 0
    tiles_per_seq = seq // tmc
    n_x_tiles = batch * tiles_per_seq
    n_tiles = n_x_tiles + (batch if with_ctx else 0)
    hpt = tmc // HALO
    n_hblk = t // HALO
    w_pad = jnp.zeros((32, ch), F32).at[:n_taps].set(conv_w.astype(F32))

    main = lambda off: pl.BlockSpec((tmc, ch), lambda i: (i, off // ch))
    prev = lambda off: pl.BlockSpec((HALO, ch), lambda i: (jnp.maximum(i * hpt - 1, 0), off // ch))
    nxt = lambda off: pl.BlockSpec((HALO, ch), lambda i: (jnp.minimum((i + 1) * hpt, n_hblk - 1), off // ch))
    vec = lambda: pl.BlockSpec((1, ch), lambda i: (0, 0))
    return pl.pallas_call(
        functools.partial(_conv_kernel, tiles_per_seq=tiles_per_seq, n_x_tiles=n_x_tiles, n_taps=n_taps,
                          tmc=tmc, sub=64),
        grid=(n_tiles,),
        in_specs=[main(a_off), main(g_off), prev(a_off), prev(g_off), nxt(a_off), nxt(g_off),
                  pl.BlockSpec((32, ch), lambda i: (0, 0)), vec(), vec(), vec()],
        out_specs=pl.BlockSpec((tmc, ch), lambda i: (i, 0)),
        out_shape=jax.ShapeDtypeStruct((t if with_ctx else batch * seq, ch), BF16),
        scratch_shapes=[pltpu.VMEM((tmc + 2 * HALO, ch), F32), pltpu.VMEM((7, tmc + 2 * HALO, ch), F32),
                        pltpu.VMEM((tmc, ch), F32)],
        compiler_params=_cparams(("arbitrary",), VMEM_LIMIT),
        name="conformer_conv",
    )(px, px, px, px, px, px, w_pad, conv_b.reshape(1, ch).astype(F32), ln_g.reshape(1, ch).astype(F32),
      ln_b.reshape(1, ch).astype(F32))


def _outproj_kernel(*refs, d, n_in, n_exp):
    x_ref, mod_ref, g_ref, rwc_ref, rwh_ref = refs[:5]
    a_refs = refs[5:5 + n_in]
    w_refs = refs[5 + n_in:5 + 2 * n_in]
    xo_ref, hp_ref, lt_ref = refs[5 + 2 * n_in:]
    o = jnp.dot(a_refs[0][...], w_refs[0][...], preferred_element_type=F32)
    for a_ref, w_ref in zip(a_refs[1:], w_refs[1:]):
        o = o + jnp.dot(a_ref[...], w_ref[...], preferred_element_type=F32)
    x_new = x_ref[...] + mod_ref[0, :, 2 * d:3 * d] * o
    xo_ref[...] = x_new
    hf = _rms(x_new, g_ref[...]) * (1.0 + mod_ref[0, :, 4 * d:5 * d]) + mod_ref[0, :, 3 * d:4 * d]
    hp_ref[...] = _pack_pairs(hf)
    h_hi = hf.astype(BF16)
    h_lo = (hf - h_hi.astype(F32)).astype(BF16)
    both = jnp.dot(h_hi, rwc_ref[...], preferred_element_type=F32)
    lg = both[:, :LANES] + both[:, LANES:] + jnp.dot(h_lo, rwh_ref[...], preferred_element_type=F32)
    lt_ref[...] = lg.T[:n_exp]


def _outproj(tok, mod3, layer, norm_g, rw_cat, rw_hi, acts, w, w_layer, *, t_act, n_exp, n_x_tiles, tiles_per_seq, tm):
    d = tok.shape[1]
    n_batch = n_x_tiles // tiles_per_seq
    n_in = len(acts)

    def mod_idx(i):
        return (layer * 8 + jnp.where(i < n_x_tiles, i // tiles_per_seq, n_batch), 0, 0)

    in_specs = [pl.BlockSpec((tm, d), lambda i: (i, 0)),
                pl.BlockSpec((1, 1, mod3.shape[2]), mod_idx),
                pl.BlockSpec((1, d), lambda i: (0, 0)),
                pl.BlockSpec(rw_cat.shape, lambda i: (0, 0)),
                pl.BlockSpec(rw_hi.shape, lambda i: (0, 0))]
    in_specs += [pl.BlockSpec((tm, a.shape[1]), lambda i: (i, 0)) for a in acts]
    kw = acts[0].shape[1]
    assert all(a.shape[1] == kw for a in acts) and w.shape[1] == kw * n_in
    in_specs += [pl.BlockSpec((None, kw, d), lambda i, r=r: (w_layer, r, 0)) for r in range(n_in)]
    return pl.pallas_call(
        functools.partial(_outproj_kernel, d=d, n_in=n_in, n_exp=n_exp),
        grid=(t_act // tm,),
        in_specs=in_specs,
        out_specs=[pl.BlockSpec((tm, d), lambda i: (i, 0)),
                   pl.BlockSpec((tm, d // 2), lambda i: (i, 0)),
                   pl.BlockSpec((n_exp, tm), lambda i: (0, i))],
        out_shape=[jax.ShapeDtypeStruct((t_act, d), F32),
                   jax.ShapeDtypeStruct((t_act, d // 2), I32),
                   jax.ShapeDtypeStruct((n_exp, t_act), F32)],
        compiler_params=_cparams(("arbitrary",), VMEM_LIMIT),
        name="out_proj",
    )(tok, mod3, norm_g.reshape(1, d), rw_cat, rw_hi, *acts, *([w] * n_in))


def _route_tile(logits, bias, n_exp):
    epg = n_exp // N_GROUPS
    aff = jax.nn.sigmoid(logits)
    sel = aff + bias
    row = lambda a, i: a[i:i + 1, :]

    scores = []
    for g in range(N_GROUPS):
        best = None
        for i in range(epg):
            for j in range(i + 1, epg):
                pair = row(sel, g * epg + i) + row(sel, g * epg + j)
                best = pair if best is None else jnp.maximum(best, pair)
        scores.append(best)
    grp = jnp.zeros(scores[0].shape, I32)
    top = scores[0]
    for g in range(1, N_GROUPS):
        better = scores[g] > top
        grp = jnp.where(better, g, grp)
        top = jnp.where(better, scores[g], top)

    def pick(a, i):
        out = row(a, i)
        for g in range(1, N_GROUPS):
            out = jnp.where(grp == g, row(a, g * epg + i), out)
        return out

    v = [pick(sel, i) for i in range(epg)]
    a = [pick(aff, i) for i in range(epg)]
    ranks = []
    for i in range(epg):
        r = jnp.zeros(grp.shape, I32)
        for j in range(epg):
            if j != i:
                ahead = (v[j] > v[i]) | ((v[j] == v[i]) & (j < i)) if j < i else (v[j] > v[i])
                r = r + ahead.astype(I32)
        ranks.append(r)
    zero_i, zero_f = jnp.zeros(grp.shape, I32), jnp.zeros(grp.shape, F32)
    experts, affs = [], []
    for k in range(TOP_K):
        e_k, a_k = zero_i, zero_f
        for i in range(epg):
            hit = ranks[i] == k
            e_k = jnp.where(hit, i, e_k)
            a_k = jnp.where(hit, a[i], a_k)
        experts.append(grp * epg + e_k)
        affs.append(a_k)
    den = affs[0] + affs[1]
    return experts, [a_k / den for a_k in affs]


def _route_plan_kernel(l_ref, b_ref, tri_ref, low_ref, g_ref, pos_ref, be_ref, nu_ref, cnt_scr, carry_scr,
                       *, n_exp, tt, be_offset):
    ph = pl.program_id(0)
    i = pl.program_id(1)
    experts, gates = _route_tile(l_ref[...], b_ref[...], n_exp)
    eid = lax.broadcasted_iota(I32, (n_exp, tt), 0)
    onehot = [(eid == e_k).astype(F32) for e_k in experts]
    both = onehot[0] + onehot[1]
    tile_cnt = jnp.sum(both, axis=1, keepdims=True)

    @pl.when(ph == 0)
    def _():
        @pl.when(i == 0)
        def _():
            cnt_scr[...] = jnp.zeros(cnt_scr.shape, F32)
        cnt_scr[...] += tile_cnt

    @pl.when(ph == 1)
    def _():
        @pl.when(i == 0)
        def _():
            carry_scr[...] = jnp.zeros(carry_scr.shape, F32)
        blocks = jnp.floor((cnt_scr[...] + (MOE_BLK - 1)) * (1.0 / MOE_BLK))
        blocks_b = jnp.broadcast_to(blocks, (n_exp, LANES)).astype(BF16)
        first_blk = jnp.dot(low_ref[...], blocks_b, preferred_element_type=F32)[:, 0:1]
        prefix = jnp.dot(both.astype(BF16), tri_ref[...], preferred_element_type=F32)
        slot = first_blk * MOE_BLK + carry_scr[...] + prefix
        carry_scr[...] += tile_cnt
        pos = [jnp.sum(oh * slot, axis=0, keepdims=True).astype(I32) for oh in onehot]
        for q in range(tt // ROW_TILE):
            for k in range(TOP_K):
                pos_ref[q, :, k * ROW_TILE:(k + 1) * ROW_TILE] = pos[k][:, q * ROW_TILE:(q + 1) * ROW_TILE]
        g_ref[...] = jnp.concatenate(gates + [jnp.zeros_like(gates[0])] * (8 - TOP_K), axis=0)
        last_blk = first_blk + blocks
        bidx = lax.broadcasted_iota(I32, (n_exp, be_ref.shape[1]), 1).astype(F32)
        owner = jnp.sum((last_blk <= bidx).astype(F32), axis=0, keepdims=True)
        be_ref[...] = jnp.minimum(owner, n_exp - 1.0).astype(I32) + be_offset
        nu_ref[...] = jnp.broadcast_to(jnp.sum(blocks, axis=0, keepdims=True), nu_ref.shape).astype(I32)


def _route_plan(logits_t, router_b, be_offset):
    n_exp, t = logits_t.shape
    tt = next(m for m in (1024, 512, 256) if t % m == 0)
    n_tiles = t // tt
    nb = -(-(t * TOP_K + n_exp * (MOE_BLK - 1)) // MOE_BLK)
    nb_pad = -(-nb // LANES) * LANES
    tok_i = np.arange(tt)
    tri = jnp.asarray(tok_i[:, None] < tok_i[None, :], BF16)
    exp_i = np.arange(n_exp)
    low = jnp.asarray(exp_i[None, :] < exp_i[:, None], BF16)
    const = lambda shape: pl.BlockSpec(shape, lambda ph, i: (0,) * len(shape))
    gates, pos3, be, nu = pl.pallas_call(
        functools.partial(_route_plan_kernel, n_exp=n_exp, tt=tt, be_offset=be_offset),
        grid=(2, n_tiles),
        in_specs=[pl.BlockSpec((n_exp, tt), lambda ph, i: (0, i)),
                  const((n_exp, 1)), const((tt, tt)), const((n_exp, n_exp))],
        out_specs=[pl.BlockSpec((8, tt), lambda ph, i: (0, i * ph)),
                   pl.BlockSpec((tt // ROW_TILE, 1, TOP_K * ROW_TILE), lambda ph, i: (i * ph, 0, 0)),
                   const((1, nb_pad)), const((1, LANES))],
        out_shape=[jax.ShapeDtypeStruct((8, t), F32),
                   jax.ShapeDtypeStruct((t // ROW_TILE, 1, TOP_K * ROW_TILE), I32),
                   jax.ShapeDtypeStruct((1, nb_pad), I32),
                   jax.ShapeDtypeStruct((1, LANES), I32)],
        scratch_shapes=[pltpu.VMEM((n_exp, 1), F32), pltpu.VMEM((n_exp, 1), F32)],
        compiler_params=_cparams(("arbitrary", "arbitrary"), VMEM_LIMIT),
        name="route_plan",
    )(logits_t, router_b.reshape(n_exp, 1).astype(F32), tri, low)
    return gates, pos3, be, nu, nb


def _row_dma_loop(n_rows, make_copies):
    def body(it, carry):
        for u in range(DMA_UNROLL):
            for cp in make_copies(it * DMA_UNROLL + u):
                cp.start()
        return carry
    lax.fori_loop(0, n_rows // DMA_UNROLL, body, 0)


def _dispatch_kernel(pos_ref, hp_ref, xs_in_ref, xs_ref, sbuf, sem, *, n_tiles):
    del xs_in_ref
    i = pl.program_id(0)
    slot = i % 2

    def wait_slot(s):
        for _ in range(TOP_K):
            pltpu.make_async_copy(sbuf.at[s], xs_ref.at[pl.ds(0, ROW_TILE)], sem.at[s]).wait()

    @pl.when(i >= 2)
    def _():
        wait_slot(slot)

    sbuf[slot] = hp_ref[...]

    def copies(r):
        return [pltpu.make_async_copy(sbuf.at[slot, pl.ds(r, 1)],
                                      xs_ref.at[pl.ds(pos_ref[0, 0, k * ROW_TILE + r], 1)], sem.at[slot])
                for k in range(TOP_K)]

    _row_dma_loop(ROW_TILE, copies)

    @pl.when(i == n_tiles - 1)
    def _():
        wait_slot(slot)
        if n_tiles >= 2:
            wait_slot(1 - slot)


def _dispatch(hp, pos3, n_slots):
    t, half = hp.shape
    n_tiles = t // ROW_TILE
    xs0 = jnp.zeros((n_slots, half), I32)
    return pl.pallas_call(
        functools.partial(_dispatch_kernel, n_tiles=n_tiles),
        grid=(n_tiles,),
        in_specs=[pl.BlockSpec((1, 1, TOP_K * ROW_TILE), lambda i: (i, 0, 0), memory_space=pltpu.SMEM),
                  pl.BlockSpec((ROW_TILE, half), lambda i: (i, 0)),
                  pl.BlockSpec(memory_space=pl.ANY)],
        out_specs=pl.BlockSpec(memory_space=pl.ANY),
        out_shape=jax.ShapeDtypeStruct((n_slots, half), I32),
        scratch_shapes=[pltpu.VMEM((2, ROW_TILE, half), I32), pltpu.SemaphoreType.DMA((2,))],
        input_output_aliases={2: 0},
        compiler_params=_cparams(("arbitrary",), VMEM_LIMIT),
        name="moe_dispatch",
    )(pos3, hp, xs0)


def _moe_kernel(be_ref, nu_ref, x_ref, wg_ref, wu_ref, wd_ref, *rest, cast):
    b = pl.program_id(0)
    half = x_ref.shape[1]
    n_used = nu_ref[0, 0]
    if cast is None:
        (y_ref,) = rest
    else:
        src = rest[0:3]
        y_ref = rest[3]
        dst = rest[4:7]
        stage, ostage = rest[7:10], rest[10:13]
        sem_in, sem_out = rest[13], rest[14]
        n_chunks, chunk_rows, row0 = cast
        slot = b % 2

        def in_copies(chunk, s):
            return [pltpu.make_async_copy(
                src[k].at[pl.ds(pl.multiple_of(row0[k] + chunk * chunk_rows[k], 16), chunk_rows[k])],
                stage[k].at[s], sem_in.at[s]) for k in range(3)]

        def out_copies(chunk, s):
            return [pltpu.make_async_copy(
                ostage[k].at[s], dst[k].at[pl.ds(pl.multiple_of(chunk * chunk_rows[k], 16), chunk_rows[k])],
                sem_out.at[s]) for k in range(3)]

        def chunk_of(step):
            return jnp.where(step < n_chunks, step, n_chunks - 2 + step % 2)

        this_chunk = chunk_of(b)
        next_chunk = chunk_of(b + 1)

    @pl.when(b < n_used)
    def _():
        if cast is not None:
            @pl.when(b == 0)
            def _():
                for cp in in_copies(0, 0):
                    cp.start()

            for cp in in_copies(this_chunk, slot):
                cp.wait()

            @pl.when(b >= 2)
            def _():
                for cp in out_copies(this_chunk, slot):
                    cp.wait()

            for cp in in_copies(next_chunk, 1 - slot):
                cp.start()
            for k in range(3):
                ostage[k][slot] = stage[k][slot].astype(BF16)

        u = x_ref[...]
        x_hi = _unpack_hi(u).astype(BF16)
        x_lo = _unpack_lo(u).astype(BF16)

        def proj(w_ref):
            return (jnp.dot(x_hi, w_ref[0, :half, :], preferred_element_type=F32)
                    + jnp.dot(x_lo, w_ref[0, half:, :], preferred_element_type=F32))

        act = (_silu(proj(wg_ref)) * proj(wu_ref)).astype(BF16)
        y_ref[...] = _pack_pairs(jnp.dot(act, wd_ref[0], preferred_element_type=F32))

        if cast is not None:
            for cp in out_copies(this_chunk, slot):
                cp.start()

            @pl.when(b == n_used - 1)
            def _():
                for cp in out_copies(this_chunk, slot) + in_copies(next_chunk, 1 - slot):
                    cp.wait()

                @pl.when(b >= 1)
                def _():
                    for cp in out_copies(this_chunk, 1 - slot):
                        cp.wait()

    @pl.when(b >= n_used)
    def _():
        y_ref[...] = jnp.zeros(y_ref.shape, y_ref.dtype)


def _moe(xs, be, n_used, wg, wu, wd, next_f32=None, next_layer=0, min_used=0):
    n_slots, half = xs.shape
    nb = n_slots // MOE_BLK
    n_exp, d, d_exp = wg.shape
    w_idx = lambda b, be, nu: (be[0, b], 0, 0)
    in_specs = [pl.BlockSpec((MOE_BLK, half), lambda b, be, nu: (b, 0)),
                pl.BlockSpec((1, d, d_exp), w_idx),
                pl.BlockSpec((1, d, d_exp), w_idx),
                pl.BlockSpec((1, d_exp, d), w_idx)]
    out_specs = [pl.BlockSpec((MOE_BLK, half), lambda b, be, nu: (b, 0))]
    out_shape = [jax.ShapeDtypeStruct((n_slots, half), I32)]
    scratch, cast, extra = [], None, []
    if next_f32 is not None:
        n_chunks = 1 << (min_used.bit_length() - 1)
        rows = (n_exp * d, n_exp * d, n_exp * d_exp)
        widths = (d_exp, d_exp, d)
        chunk_rows = tuple(r // n_chunks for r in rows)
        assert all(r % n_chunks == 0 and c % 16 == 0 for r, c in zip(rows, chunk_rows)) and 2 <= n_chunks <= min_used
        cast = (n_chunks, chunk_rows, tuple(next_layer * r for r in rows))
        extra = [w.reshape(-1, wd_) for w, wd_ in zip(next_f32, widths)]
        in_specs += [pl.BlockSpec(memory_space=pl.ANY)] * 3
        out_specs += [pl.BlockSpec(memory_space=pl.ANY)] * 3
        out_shape += [jax.ShapeDtypeStruct((r, wd_), BF16) for r, wd_ in zip(rows, widths)]
        scratch = ([pltpu.VMEM((2, c, wd_), F32) for c, wd_ in zip(chunk_rows, widths)]
                   + [pltpu.VMEM((2, c, wd_), BF16) for c, wd_ in zip(chunk_rows, widths)]
                   + [pltpu.SemaphoreType.DMA((2,)), pltpu.SemaphoreType.DMA((2,))])
    grid_spec = pltpu.PrefetchScalarGridSpec(
        num_scalar_prefetch=2, grid=(nb,), in_specs=in_specs, out_specs=out_specs, scratch_shapes=scratch)
    outs = pl.pallas_call(
        functools.partial(_moe_kernel, cast=cast),
        grid_spec=grid_spec,
        out_shape=out_shape,
        compiler_params=_cparams(("arbitrary",), MOE_VMEM_LIMIT),
        name="moe_experts",
    )(be, n_used, xs, wg, wu, wd, *extra)
    if next_f32 is None:
        return outs[0], None
    shapes = ((n_exp, d, d_exp), (n_exp, d, d_exp), (n_exp, d_exp, d))
    return outs[0], tuple(o.reshape(s) for o, s in zip(outs[1:], shapes))


def _combine_kernel(pos_ref, posn_ref, x_ref, mod_ref, gt_ref, y_hbm, o_ref, ybuf, sem, *, d, n_tiles):
    i = pl.program_id(0)
    slot = i % 2
    half = d // 2

    def issue(p_ref, s):
        def copies(r):
            return [pltpu.make_async_copy(y_hbm.at[pl.ds(p_ref[0, 0, k * ROW_TILE + r], 1)],
                                          ybuf.at[s, k, pl.ds(r, 1)], sem.at[s])
                    for k in range(TOP_K)]
        _row_dma_loop(ROW_TILE, copies)

    @pl.when(i == 0)
    def _():
        issue(pos_ref, 0)

    @pl.when(i + 1 < n_tiles)
    def _():
        issue(posn_ref, 1 - slot)

    for k in range(TOP_K):
        pltpu.make_async_copy(y_hbm.at[pl.ds(0, ROW_TILE)], ybuf.at[slot, k], sem.at[slot]).wait()

    u0, u1 = ybuf[slot, 0], ybuf[slot, 1]
    w0, w1 = gt_ref[:, 0:1], gt_ref[:, 1:2]
    f_hi = w0 * _unpack_hi(u0) + w1 * _unpack_hi(u1)
    f_lo = w0 * _unpack_lo(u0) + w1 * _unpack_lo(u1)
    o_ref[:, :half] = x_ref[:, :half] + mod_ref[0, :, 5 * d:5 * d + half] * f_hi
    o_ref[:, half:] = x_ref[:, half:] + mod_ref[0, :, 5 * d + half:6 * d] * f_lo


def _combine(x_mid, mod3, layer, gates, pos3, y, *, n_x_tiles, tiles_per_seq):
    t, d = x_mid.shape
    n_batch = n_x_tiles // tiles_per_seq
    n_tiles = t // ROW_TILE
    half = d // 2

    def mod_idx(i):
        return (layer * 8 + jnp.where(i < n_x_tiles, i // tiles_per_seq, n_batch), 0, 0)

    smem_blk = lambda f: pl.BlockSpec((1, 1, TOP_K * ROW_TILE), f, memory_space=pltpu.SMEM)
    return pl.pallas_call(
        functools.partial(_combine_kernel, d=d, n_tiles=n_tiles),
        grid=(n_tiles,),
        in_specs=[smem_blk(lambda i: (i, 0, 0)),
                  smem_blk(lambda i: (jnp.minimum(i + 1, n_tiles - 1), 0, 0)),
                  pl.BlockSpec((ROW_TILE, d), lambda i: (i, 0)),
                  pl.BlockSpec((1, 1, mod3.shape[2]), mod_idx),
                  pl.BlockSpec((ROW_TILE, TOP_K), lambda i: (i, 0)),
                  pl.BlockSpec(memory_space=pl.ANY)],
        out_specs=pl.BlockSpec((ROW_TILE, d), lambda i: (i, 0)),
        out_shape=jax.ShapeDtypeStruct((t, d), F32),
        scratch_shapes=[pltpu.VMEM((2, TOP_K, ROW_TILE, half), I32), pltpu.SemaphoreType.DMA((2,))],
        compiler_params=_cparams(("arbitrary",), VMEM_LIMIT),
        name="moe_combine",
    )(pos3, pos3, x_mid, mod3, gates, y)


def _rope_tables(seq, tm):
    nf = HEAD_DIM // 4
    inv = jnp.power(ROPE_BASE, -jnp.arange(nf, dtype=F32) / nf)
    tt = jnp.arange(seq)
    ar = (tt // GRID_W).astype(F32)[:, None] * inv
    ac = (tt % GRID_W).astype(F32)[:, None] * inv
    ang = jnp.concatenate([ar, ar, ac, ac], axis=-1)
    cos = jnp.concatenate([jnp.cos(ang), jnp.ones((tm, HEAD_DIM), F32)], axis=0)
    sin = jnp.concatenate([jnp.sin(ang), jnp.zeros((tm, HEAD_DIM), F32)], axis=0)
    return cos, sin


def kernel(x, c, ctx, c_ctx, ada_w, ada_b, norm_mix_g, norm_ffn_g, ab_w_in, ab_w_out, ab_q_norm, ab_k_norm, ab_sink, conv_w, conv_b, conv_ln_g, conv_ln_b, na_w_in, na_w_out, na_q_norm, na_k_norm, na_rpb, router_w, router_b, moe_w_gate, moe_w_up, moe_w_down):
    batch, seq, d = x.shape
    ctx_len = ctx.shape[1]
    depth = ada_w.shape[0]
    n_exp = router_w.shape[1]
    b_ch = conv_w.shape[-1]
    a_qw = ab_w_out.shape[1] - b_ch
    a_kvw = (ab_w_in.shape[-1] - a_qw - 2 * b_ch) // 2
    n_kv = 2
    na_w = na_w_out.shape[1]
    assert batch + 1 <= 8 and a_qw == b_ch and n_exp % 8 == 0

    tx, tc = batch * seq, batch * ctx_len
    t = tx + tc
    tm_in = next(m for m in (1024, 512, 256) if seq % m == 0 and tc % m == 0)
    tn = 512 if d >= 2048 else 256

    c8 = jnp.zeros((8, d), F32).at[:batch].set(c).at[batch].set(c_ctx)
    mod3 = _ada_all(c8, ada_w, ada_b).reshape(depth * 8, 1, 6 * d)

    cos, sin = _rope_tables(seq, tm_in)
    rw = jnp.zeros((d, LANES), F32).at[:, :n_exp].set(router_w.astype(F32))
    rw_hi = rw.astype(BF16)
    rw_cat = jnp.concatenate([rw_hi, (rw - rw_hi.astype(F32)).astype(BF16)], axis=1)
    rows = seq // GRID_W

    k0, v0, u0 = a_qw, a_qw + a_kvw, a_qw + 2 * a_kvw
    k_off, v_off = a_qw + 2 * b_ch, a_qw + 2 * b_ch + a_kvw
    ab_w_in_b = jnp.concatenate([ab_w_in[..., :k0], ab_w_in[..., u0:], ab_w_in[..., k0:u0]], axis=-1).astype(BF16)
    ab_w_out_b = ab_w_out.astype(BF16)
    na_w_in_b = na_w_in.astype(BF16)
    na_w_out_b = na_w_out.astype(BF16)
    d_exp = moe_w_gate.shape[-1]
    moe_f32 = (moe_w_gate, moe_w_up, moe_w_down)
    moe_w = tuple(w[0].astype(BF16) for w in moe_f32)

    def ab_kind(col):
        return "q" if col < a_qw else ("k" if k_off <= col < v_off else "p")

    def na_kind(col):
        return "q" if col < na_w else ("k" if col < 2 * na_w else "p")

    tok = jnp.concatenate([x.reshape(tx, d), ctx.reshape(tc, d)], axis=0)

    for i in range(depth):
        with_ctx = i < depth - 1
        j = i // 2
        t_act = t if with_ctx else tx
        tiles = dict(n_x_tiles=tx // tm_in, tiles_per_seq=seq // tm_in)
        if i % 2 == 0:
            px = _inproj(tok, mod3, i, norm_mix_g[i], ab_w_in_b, j, ab_q_norm[j], ab_k_norm[j], ab_kind, cos, sin,
                         tm=tm_in, tn=tn, **tiles)
            att = _win_attn(px, ab_sink[j], batch=batch, seq=seq, ctx_len=ctx_len, a_qw=a_qw, n_kv=n_kv,
                            k_off=k_off, v_off=v_off, with_ctx=with_ctx)
            cv = _conv(px, conv_w[j], conv_b[j], conv_ln_g[j], conv_ln_b[j], a_off=a_qw, g_off=a_qw + b_ch,
                       batch=batch, seq=seq, ctx_len=ctx_len, with_ctx=with_ctx, tmc=256)
            acts, w_out = [att, cv], ab_w_out_b
        else:
            px = _inproj(tok, mod3, i, norm_mix_g[i], na_w_in_b, j, na_q_norm[j], na_k_norm[j], na_kind, None, None,
                         tm=tm_in, tn=tn, **tiles)
            bias = _na_bias_table(na_rpb[j], rows)
            att = _na_attn(px, bias, batch=batch, seq=seq, ctx_len=ctx_len, n_heads=na_w // HEAD_DIM,
                           with_ctx=with_ctx)
            acts, w_out = [att], na_w_out_b

        otiles = dict(n_x_tiles=tx // ROW_TILE, tiles_per_seq=seq // ROW_TILE)
        x_mid, hp, logits_t = _outproj(tok, mod3, i, norm_ffn_g[i], rw_cat, rw_hi, acts, w_out, j, t_act=t_act,
                                       n_exp=n_exp, tm=ROW_TILE, **otiles)
        g_out, pos3, be, n_used, nb = _route_plan(logits_t, router_b, 0)
        xs = _dispatch(hp, pos3, nb * MOE_BLK)
        if i + 1 < depth:
            y, moe_w = _moe(xs, be, n_used, *moe_w, next_f32=moe_f32, next_layer=i + 1,
                            min_used=-(-t_act * TOP_K // MOE_BLK))
        else:
            y, _ = _moe(xs, be, n_used, *moe_w)
        tok = _combine(x_mid, mod3, i, g_out[:TOP_K].T, pos3, y, **otiles)

    return tok[:tx].reshape(batch, seq, d)
```

```python
import functools

import numpy as np
import jax
import jax.numpy as jnp
from jax import lax
from jax.experimental import pallas as pl
from jax.experimental.pallas import tpu as pltpu

F32 = jnp.float32
BF16 = jnp.bfloat16
I32 = jnp.int32

HEAD_DIM = 128
LANES = 128
GRID_W = 64
WINDOW = 128
N_GROUPS = 4
TOP_K = 2
ROPE_BASE = 10000.0
EPS = 1e-6
NEG_INF = -1e30
MOE_BLK = 256
ROW_TILE = 256
NA_RB = 4
NA_HG = 4
HALO = 16
DMA_UNROLL = 64
VMEM_LIMIT = 56 * 1024 * 1024
HI_MASK = -65536
LOG2E = 1.4426950408889634
Q_SCALE = HEAD_DIM ** -0.5 * LOG2E


def _cparams(sem, vmem=None):
    return pltpu.CompilerParams(dimension_semantics=sem, vmem_limit_bytes=vmem)


def _silu(v):
    return v * jax.nn.sigmoid(v)


def _rms(v, g):
    ms = jnp.mean(v * v, axis=-1, keepdims=True)
    return v * lax.rsqrt(ms + EPS) * g


def _pack_pairs(v):
    half = v.shape[1] // 2
    bits = pltpu.bitcast(v.astype(BF16).astype(F32), I32)
    return (bits[:, :half] & HI_MASK) | lax.shift_right_logical(bits[:, half:], 16)


def _unpack_hi(u):
    return pltpu.bitcast(u & HI_MASK, F32)


def _unpack_lo(u):
    return pltpu.bitcast(lax.shift_left(u, 16), F32)


def _ada_kernel(c_ref, w_ref, b_ref, o_ref):
    sc = _silu(c_ref[...])
    o_ref[0] = jnp.dot(sc.astype(BF16), w_ref[0].astype(BF16), preferred_element_type=F32) + b_ref[0]


def _ada_all(c8, ada_w, ada_b):
    depth, d, n = ada_w.shape
    tn = min(n, 1024)
    return pl.pallas_call(
        _ada_kernel,
        grid=(depth, n // tn),
        in_specs=[pl.BlockSpec((8, d), lambda l, j: (0, 0)),
                  pl.BlockSpec((1, d, tn), lambda l, j: (l, 0, j)),
                  pl.BlockSpec((1, 1, tn), lambda l, j: (l, 0, j))],
        out_specs=pl.BlockSpec((1, 8, tn), lambda l, j: (l, 0, j)),
        out_shape=jax.ShapeDtypeStruct((depth, 8, n), F32),
        compiler_params=_cparams(("arbitrary", "arbitrary"), VMEM_LIMIT),
        name="ada_mod",
    )(c8, ada_w, ada_b.reshape(depth, 1, n))


def _rope(y, cos, sin):
    lane = lax.broadcasted_iota(I32, y.shape, 1)
    first = (lane & 32) == 0
    fwd = pltpu.roll(y, 32, 1)
    bwd = pltpu.roll(y, 96, 1)
    return y * cos + jnp.where(first, -bwd, fwd) * sin


def _inproj_kernel(*refs, d, tn, groups, rope):
    if rope:
        x_ref, mod_ref, g_ref, w_ref, qg_ref, kg_ref, cos_ref, sin_ref, o_ref, h_scr = refs
    else:
        x_ref, mod_ref, g_ref, w_ref, qg_ref, kg_ref, o_ref, h_scr = refs
        cos_ref = sin_ref = None
    j = pl.program_id(1)

    @pl.when(j == 0)
    def _():
        y = _rms(x_ref[...], g_ref[...])
        shift = mod_ref[0, :, 0:d]
        scale = mod_ref[0, :, d:2 * d]
        h_scr[...] = (y * (1.0 + scale) + shift).astype(BF16)

    acc = jnp.dot(h_scr[...], w_ref[...], preferred_element_type=F32)

    for kinds, lo, hi in groups:
        @pl.when((j >= lo) & (j <= hi))
        def _(kinds=kinds):
            if all(k == "p" for k in kinds):
                o_ref[...] = acc.astype(o_ref.dtype)
                return
            for s, kind in enumerate(kinds):
                piece = acc[:, s * LANES:(s + 1) * LANES]
                if kind != "p":
                    piece = _rms(piece, (qg_ref if kind == "q" else kg_ref)[...])
                    if rope:
                        piece = _rope(piece, cos_ref[...], sin_ref[...])
                    if kind == "q":
                        piece = piece * Q_SCALE
                o_ref[:, s * LANES:(s + 1) * LANES] = piece.astype(o_ref.dtype)


def _inproj(tok, mod3, layer, norm_g, w, w_layer, qg, kg, kind_of_col, cos, sin, *, n_x_tiles, tiles_per_seq, tm, tn):
    t, d = tok.shape
    n = w.shape[2]
    nj = n // tn
    per_j = [tuple(kind_of_col(j * tn + s * LANES) for s in range(tn // LANES)) for j in range(nj)]
    groups = []
    for j, kinds in enumerate(per_j):
        if groups and groups[-1][0] == kinds and groups[-1][2] == j - 1:
            groups[-1] = (kinds, groups[-1][1], j)
        else:
            groups.append((kinds, j, j))
    rope = cos is not None
    n_batch = n_x_tiles // tiles_per_seq

    def mod_idx(i, j):
        return (layer * 8 + jnp.where(i < n_x_tiles, i // tiles_per_seq, n_batch), 0, 0)

    def pos_idx(i, j):
        return (jnp.where(i < n_x_tiles, i % tiles_per_seq, tiles_per_seq), 0)

    in_specs = [pl.BlockSpec((tm, d), lambda i, j: (i, 0)),
                pl.BlockSpec((1, 1, mod3.shape[2]), mod_idx),
                pl.BlockSpec((1, d), lambda i, j: (0, 0)),
                pl.BlockSpec((None, d, tn), lambda i, j: (w_layer, 0, j)),
                pl.BlockSpec((1, HEAD_DIM), lambda i, j: (0, 0)),
                pl.BlockSpec((1, HEAD_DIM), lambda i, j: (0, 0))]
    args = [tok, mod3, norm_g.reshape(1, d), w, qg.reshape(1, HEAD_DIM), kg.reshape(1, HEAD_DIM)]
    if rope:
        in_specs += [pl.BlockSpec((tm, HEAD_DIM), pos_idx), pl.BlockSpec((tm, HEAD_DIM), pos_idx)]
        args += [cos, sin]
    return pl.pallas_call(
        functools.partial(_inproj_kernel, d=d, tn=tn, groups=tuple(groups), rope=rope),
        grid=(t // tm, nj),
        in_specs=in_specs,
        out_specs=pl.BlockSpec((tm, tn), lambda i, j: (i, j)),
        out_shape=jax.ShapeDtypeStruct((t, n), BF16),
        scratch_shapes=[pltpu.VMEM((tm, d), BF16)],
        compiler_params=_cparams(("arbitrary", "arbitrary"), VMEM_LIMIT),
        name="in_proj",
    )(*args)


def _softmax_pv(s, v, extra_logit=None):
    m = jnp.max(s, axis=-1, keepdims=True)
    if extra_logit is not None:
        m = jnp.maximum(m, extra_logit)
    p = jnp.exp2(s - m)
    den = jnp.sum(p, axis=-1, keepdims=True)
    if extra_logit is not None:
        den = den + jnp.exp2(extra_logit - m)
    o = jnp.dot(p.astype(BF16), v, preferred_element_type=F32)
    return o / den


def _win_attn_kernel(sink_ref, q_ref, kp_ref, kc_ref, kn_ref, vp_ref, vc_ref, vn_ref, kx_ref, vx_ref, o_ref,
                     *, n_grp, nb):
    h = pl.program_id(1)
    n = pl.program_id(2)
    w = WINDOW
    q = q_ref[...]
    qs = jnp.concatenate([q[:, g * HEAD_DIM:(g + 1) * HEAD_DIM] for g in range(n_grp)], axis=0)
    k = jnp.concatenate([kp_ref[...], kc_ref[...], kn_ref[...], kx_ref[...]], axis=0)
    v = jnp.concatenate([vp_ref[...], vc_ref[...], vn_ref[...], vx_ref[...]], axis=0)
    s = lax.dot_general(qs, k, (((1,), (1,)), ((), ())), preferred_element_type=F32)
    rows = lax.broadcasted_iota(I32, s.shape, 0) & (w - 1)
    cols = lax.broadcasted_iota(I32, s.shape, 1)
    is_x = n < nb
    lo = jnp.where(is_x, jnp.where(n > 0, 0, w), 0)
    hi = jnp.where(is_x, jnp.where(n < nb - 1, 3 * w, 2 * w), 0)
    local_ok = (jnp.abs(cols - w - rows) <= WINDOW) & (cols >= lo) & (cols < hi)
    s = jnp.where(local_ok | (cols >= 3 * w), s, NEG_INF)
    for g in range(n_grp):
        o = _softmax_pv(s[g * w:(g + 1) * w], v, sink_ref[h, g] * LOG2E)
        o_ref[:, g * HEAD_DIM:(g + 1) * HEAD_DIM] = o.astype(o_ref.dtype)


def _win_attn(px, sink, *, batch, seq, ctx_len, a_qw, n_kv, k_off, v_off, with_ctx):
    t = px.shape[0]
    w = WINDOW
    n_grp = a_qw // HEAD_DIM // n_kv
    nb = seq // w
    nq = nb + (ctx_len // w if with_ctx else 0)
    qw = n_grp * HEAD_DIM
    ctx_blk0 = batch * seq // ctx_len

    def q_idx(b, h, n):
        return (jnp.where(n < nb, b * nb + n, batch * nb + b * (ctx_len // w) + (n - nb)), h)

    def kv_idx(off, delta):
        def f(b, h, n):
            return (b * nb + jnp.clip(n + delta, 0, nb - 1), off // HEAD_DIM + h)
        return f

    def ctx_idx(off):
        return lambda b, h, n: (ctx_blk0 + b, off // HEAD_DIM + h)

    blk = lambda f: pl.BlockSpec((w, HEAD_DIM), f)
    in_specs = [pl.BlockSpec(memory_space=pltpu.SMEM),
                pl.BlockSpec((w, qw), q_idx),
                blk(kv_idx(k_off, -1)), blk(kv_idx(k_off, 0)), blk(kv_idx(k_off, 1)),
                blk(kv_idx(v_off, -1)), blk(kv_idx(v_off, 0)), blk(kv_idx(v_off, 1)),
                pl.BlockSpec((ctx_len, HEAD_DIM), ctx_idx(k_off)),
                pl.BlockSpec((ctx_len, HEAD_DIM), ctx_idx(v_off))]
    return pl.pallas_call(
        functools.partial(_win_attn_kernel, n_grp=n_grp, nb=nb),
        grid=(batch, n_kv, nq),
        in_specs=in_specs,
        out_specs=pl.BlockSpec((w, qw), q_idx),
        out_shape=jax.ShapeDtypeStruct((t if with_ctx else batch * seq, a_qw), BF16),
        compiler_params=_cparams(("arbitrary",) * 3, VMEM_LIMIT),
        name="win_attn",
    )(sink.reshape(n_kv, n_grp).astype(F32), *([px] * 9))


def _na_attn_kernel(q_ref, kp_ref, kc_ref, kn_ref, vp_ref, vc_ref, vn_ref, kx_ref, vx_ref, bias_ref, o_ref,
                    *, n_hg):
    nloc = 3 * NA_RB * GRID_W
    for hh in range(n_hg):
        cl = slice(hh * HEAD_DIM, (hh + 1) * HEAD_DIM)
        k = jnp.concatenate([kp_ref[:, cl], kc_ref[:, cl], kn_ref[:, cl], kx_ref[:, cl]], axis=0)
        v = jnp.concatenate([vp_ref[:, cl], vc_ref[:, cl], vn_ref[:, cl], vx_ref[:, cl]], axis=0)
        s = lax.dot_general(q_ref[:, cl], k, (((1,), (1,)), ((), ())), preferred_element_type=F32)
        s = jnp.concatenate([s[:, :nloc] + bias_ref[hh, 0], s[:, nloc:]], axis=1)
        o_ref[:, cl] = _softmax_pv(s, v).astype(o_ref.dtype)


def _na_bias_table(rpb, rows):
    n_heads, n_dr, n_dc = rpb.shape
    kh, kw = (n_dr + 1) // 2, (n_dc + 1) // 2
    n_rb = rows // NA_RB
    cidx = np.arange(GRID_W)
    cs = np.clip(cidx - kw // 2, 0, GRID_W - kw)
    col_ok = (cidx[None, :] >= cs[:, None]) & (cidx[None, :] < cs[:, None] + kw)
    dc_idx = np.clip(cidx[None, :] - cidx[:, None], -(kw - 1), kw - 1) + kw - 1
    a = jnp.where(col_ok[None, None], rpb.astype(F32)[:, :, dc_idx] * LOG2E, NEG_INF)
    masked = jnp.full((n_heads, GRID_W, GRID_W), NEG_INF, F32)
    classes = []
    for rb in (0, min(1, n_rb - 1), n_rb - 1):
        qrows = []
        for j in range(NA_RB):
            r = rb * NA_RB + j
            rs = int(np.clip(r - kh // 2, 0, rows - kh))
            blocks = []
            for tblk in range(3):
                for krl in range(NA_RB):
                    kr = (rb - 1 + tblk) * NA_RB + krl
                    ok = (rs <= kr < rs + kh) and (0 <= kr < rows)
                    blocks.append(a[:, kr - r + kh - 1] if ok else masked)
            qrows.append(jnp.concatenate(blocks, axis=-1))
        classes.append(jnp.concatenate(qrows, axis=1))
    classes.append(jnp.full_like(classes[0], NEG_INF))
    return jnp.stack(classes, axis=1)


def _na_attn(px, bias, *, batch, seq, ctx_len, n_heads, with_ctx):
    t = px.shape[0]
    qb = NA_RB * GRID_W
    assert ctx_len == qb, "context queries are processed as one extra query block"
    n_rb = seq // qb
    nq = n_rb + (1 if with_ctx else 0)
    na_w = n_heads * HEAD_DIM
    n_hg = min(NA_HG, n_heads)
    gw = n_hg * HEAD_DIM
    assert n_heads % n_hg == 0
    ctx_blk0 = batch * seq // ctx_len

    def q_idx(b, h, r):
        return (jnp.where(r < n_rb, b * n_rb + r, batch * n_rb + b), h)

    def kv_idx(off, delta):
        return lambda b, h, r: (b * n_rb + jnp.clip(r + delta, 0, n_rb - 1), off // gw + h)

    def ctx_idx(off):
        return lambda b, h, r: (ctx_blk0 + b, off // gw + h)

    def bias_idx(b, h, r):
        return (h, jnp.where(r == 0, 0, jnp.where(r < n_rb - 1, 1, jnp.where(r == n_rb - 1, 2, 3))), 0, 0)

    blk = lambda f: pl.BlockSpec((qb, gw), f)
    in_specs = [blk(q_idx),
                blk(kv_idx(na_w, -1)), blk(kv_idx(na_w, 0)), blk(kv_idx(na_w, 1)),
                blk(kv_idx(2 * na_w, -1)), blk(kv_idx(2 * na_w, 0)), blk(kv_idx(2 * na_w, 1)),
                pl.BlockSpec((ctx_len, gw), ctx_idx(na_w)),
                pl.BlockSpec((ctx_len, gw), ctx_idx(2 * na_w)),
                pl.BlockSpec((n_hg, 1, qb, 3 * qb), bias_idx)]
    return pl.pallas_call(
        functools.partial(_na_attn_kernel, n_hg=n_hg),
        grid=(batch, n_heads // n_hg, nq),
        in_specs=in_specs,
        out_specs=blk(q_idx),
        out_shape=jax.ShapeDtypeStruct((t if with_ctx else batch * seq, na_w), BF16),
        compiler_params=_cparams(("arbitrary",) * 3, VMEM_LIMIT),
        name="na_attn",
    )(*([px] * 9), bias)


def _conv_kernel(a_ref, g_ref, ap_ref, gp_ref, an_ref, gn_ref, w_ref, b_ref, lg_ref, lb_ref, o_ref, hbuf, hs, cbuf,
                 *, tiles_per_seq, n_x_tiles, n_taps, tmc, sub):
    i = pl.program_id(0)
    p = i % tiles_per_seq
    is_x = i < n_x_tiles
    has_prev = is_x & (p > 0)
    has_next = is_x & (p < tiles_per_seq - 1)

    def glu(a, g):
        return a.astype(F32) * jax.nn.sigmoid(g.astype(F32))

    hbuf[0:HALO, :] = jnp.where(has_prev, glu(ap_ref[...], gp_ref[...]), 0.0)
    hbuf[HALO:HALO + tmc, :] = glu(a_ref[...], g_ref[...])
    hbuf[HALO + tmc:2 * HALO + tmc, :] = jnp.where(has_next, glu(an_ref[...], gn_ref[...]), 0.0)

    n_buf = tmc + 2 * HALO
    for sh in range(1, 8):
        hs[sh - 1, 0:n_buf - 8, :] = hbuf[sh:sh + n_buf - 8, :]

    ch = a_ref.shape[1]
    first = HALO - n_taps // 2
    for c in range(ch // LANES):
        cl = slice(c * LANES, (c + 1) * LANES)
        for tb in range(tmc // sub):
            acc = jnp.zeros((sub, LANES), F32)
            for k in range(n_taps):
                sh = (first + k) % 8
                r0 = tb * sub + first + k - sh
                slab = hbuf[r0:r0 + sub, cl] if sh == 0 else hs[sh - 1, r0:r0 + sub, cl]
                acc = acc + slab * w_ref[k:k + 1, cl]
            cbuf[tb * sub:(tb + 1) * sub, cl] = acc + b_ref[:, cl]

    y = cbuf[...]
    mu = jnp.mean(y, axis=-1, keepdims=True)
    yc = y - mu
    var = jnp.mean(yc * yc, axis=-1, keepdims=True)
    yn = yc * lax.rsqrt(var + EPS) * lg_ref[...] + lb_ref[...]
    o_ref[...] = _silu(yn).astype(o_ref.dtype)


def _conv(px, conv_w, conv_b, ln_g, ln_b, *, a_off, g_off, batch, seq, ctx_len, with_ctx, tmc):
    t = px.shape[0]
    n_taps, ch = conv_w.shape
    assert n_taps // 2 <= HALO and ctx_len == tmc and seq % tmc == 0
    tiles_per_seq = seq // tmc
    n_x_tiles = batch * tiles_per_seq
    n_tiles = n_x_tiles + (batch if with_ctx else 0)
    hpt = tmc // HALO
    n_hblk = t // HALO
    w_pad = jnp.zeros((32, ch), F32).at[:n_taps].set(conv_w.astype(F32))

    main = lambda off: pl.BlockSpec((tmc, ch), lambda i: (i, off // ch))
    prev = lambda off: pl.BlockSpec((HALO, ch), lambda i: (jnp.maximum(i * hpt - 1, 0), off // ch))
    nxt = lambda off: pl.BlockSpec((HALO, ch), lambda i: (jnp.minimum((i + 1) * hpt, n_hblk - 1), off // ch))
    vec = lambda: pl.BlockSpec((1, ch), lambda i: (0, 0))
    return pl.pallas_call(
        functools.partial(_conv_kernel, tiles_per_seq=tiles_per_seq, n_x_tiles=n_x_tiles, n_taps=n_taps,
                          tmc=tmc, sub=64),
        grid=(n_tiles,),
        in_specs=[main(a_off), main(g_off), prev(a_off), prev(g_off), nxt(a_off), nxt(g_off),
                  pl.BlockSpec((32, ch), lambda i: (0, 0)), vec(), vec(), vec()],
        out_specs=pl.BlockSpec((tmc, ch), lambda i: (i, 0)),
        out_shape=jax.ShapeDtypeStruct((t if with_ctx else batch * seq, ch), BF16),
        scratch_shapes=[pltpu.VMEM((tmc + 2 * HALO, ch), F32), pltpu.VMEM((7, tmc + 2 * HALO, ch), F32),
                        pltpu.VMEM((tmc, ch), F32)],
        compiler_params=_cparams(("arbitrary",), VMEM_LIMIT),
        name="conformer_conv",
    )(px, px, px, px, px, px, w_pad, conv_b.reshape(1, ch).astype(F32), ln_g.reshape(1, ch).astype(F32),
      ln_b.reshape(1, ch).astype(F32))


def _outproj_kernel(*refs, d, n_in, n_exp):
    x_ref, mod_ref, g_ref, rwc_ref, rwh_ref = refs[:5]
    a_refs = refs[5:5 + n_in]
    w_refs = refs[5 + n_in:5 + 2 * n_in]
    xo_ref, hp_ref, lt_ref = refs[5 + 2 * n_in:]
    o = jnp.dot(a_refs[0][...], w_refs[0][...], preferred_element_type=F32)
    for a_ref, w_ref in zip(a_refs[1:], w_refs[1:]):
        o = o + jnp.dot(a_ref[...], w_ref[...], preferred_element_type=F32)
    x_new = x_ref[...] + mod_ref[0, :, 2 * d:3 * d] * o
    xo_ref[...] = x_new
    hf = _rms(x_new, g_ref[...]) * (1.0 + mod_ref[0, :, 4 * d:5 * d]) + mod_ref[0, :, 3 * d:4 * d]
    hp_ref[...] = _pack_pairs(hf)
    h_hi = hf.astype(BF16)
    h_lo = (hf - h_hi.astype(F32)).astype(BF16)
    both = jnp.dot(h_hi, rwc_ref[...], preferred_element_type=F32)
    lg = both[:, :LANES] + both[:, LANES:] + jnp.dot(h_lo, rwh_ref[...], preferred_element_type=F32)
    lt_ref[...] = lg.T[:n_exp]


def _outproj(tok, mod3, layer, norm_g, rw_cat, rw_hi, acts, w, w_layer, *, t_act, n_exp, n_x_tiles, tiles_per_seq, tm):
    d = tok.shape[1]
    n_batch = n_x_tiles // tiles_per_seq
    n_in = len(acts)

    def mod_idx(i):
        return (layer * 8 + jnp.where(i < n_x_tiles, i // tiles_per_seq, n_batch), 0, 0)

    in_specs = [pl.BlockSpec((tm, d), lambda i: (i, 0)),
                pl.BlockSpec((1, 1, mod3.shape[2]), mod_idx),
                pl.BlockSpec((1, d), lambda i: (0, 0)),
                pl.BlockSpec(rw_cat.shape, lambda i: (0, 0)),
                pl.BlockSpec(rw_hi.shape, lambda i: (0, 0))]
    in_specs += [pl.BlockSpec((tm, a.shape[1]), lambda i: (i, 0)) for a in acts]
    kw = acts[0].shape[1]
    assert all(a.shape[1] == kw for a in acts) and w.shape[1] == kw * n_in
    in_specs += [pl.BlockSpec((None, kw, d), lambda i, r=r: (w_layer, r, 0)) for r in range(n_in)]
    return pl.pallas_call(
        functools.partial(_outproj_kernel, d=d, n_in=n_in, n_exp=n_exp),
        grid=(t_act // tm,),
        in_specs=in_specs,
        out_specs=[pl.BlockSpec((tm, d), lambda i: (i, 0)),
                   pl.BlockSpec((tm, d // 2), lambda i: (i, 0)),
                   pl.BlockSpec((n_exp, tm), lambda i: (0, i))],
        out_shape=[jax.ShapeDtypeStruct((t_act, d), F32),
                   jax.ShapeDtypeStruct((t_act, d // 2), I32),
                   jax.ShapeDtypeStruct((n_exp, t_act), F32)],
        compiler_params=_cparams(("arbitrary",), VMEM_LIMIT),
        name="out_proj",
    )(tok, mod3, norm_g.reshape(1, d), rw_cat, rw_hi, *acts, *([w] * n_in))


def _route_tile(logits, bias, n_exp):
    epg = n_exp // N_GROUPS
    aff = jax.nn.sigmoid(logits)
    sel = aff + bias
    row = lambda a, i: a[i:i + 1, :]

    scores = []
    for g in range(N_GROUPS):
        best = None
        for i in range(epg):
            for j in range(i + 1, epg):
                pair = row(sel, g * epg + i) + row(sel, g * epg + j)
                best = pair if best is None else jnp.maximum(best, pair)
        scores.append(best)
    grp = jnp.zeros(scores[0].shape, I32)
    top = scores[0]
    for g in range(1, N_GROUPS):
        better = scores[g] > top
        grp = jnp.where(better, g, grp)
        top = jnp.where(better, scores[g], top)

    def pick(a, i):
        out = row(a, i)
        for g in range(1, N_GROUPS):
            out = jnp.where(grp == g, row(a, g * epg + i), out)
        return out

    v = [pick(sel, i) for i in range(epg)]
    a = [pick(aff, i) for i in range(epg)]
    ranks = []
    for i in range(epg):
        r = jnp.zeros(grp.shape, I32)
        for j in range(epg):
            if j != i:
                ahead = (v[j] > v[i]) | ((v[j] == v[i]) & (j < i)) if j < i else (v[j] > v[i])
                r = r + ahead.astype(I32)
        ranks.append(r)
    zero_i, zero_f = jnp.zeros(grp.shape, I32), jnp.zeros(grp.shape, F32)
    experts, affs = [], []
    for k in range(TOP_K):
        e_k, a_k = zero_i, zero_f
        for i in range(epg):
            hit = ranks[i] == k
            e_k = jnp.where(hit, i, e_k)
            a_k = jnp.where(hit, a[i], a_k)
        experts.append(grp * epg + e_k)
        affs.append(a_k)
    den = affs[0] + affs[1]
    return experts, [a_k / den for a_k in affs]


def _route_plan_kernel(l_ref, b_ref, tri_ref, low_ref, g_ref, pos_ref, be_ref, nu_ref, cnt_scr, carry_scr,
                       *, n_exp, tt, be_offset):
    ph = pl.program_id(0)
    i = pl.program_id(1)
    experts, gates = _route_tile(l_ref[...], b_ref[...], n_exp)
    eid = lax.broadcasted_iota(I32, (n_exp, tt), 0)
    onehot = [(eid == e_k).astype(F32) for e_k in experts]
    both = onehot[0] + onehot[1]
    tile_cnt = jnp.sum(both, axis=1, keepdims=True)

    @pl.when(ph == 0)
    def _():
        @pl.when(i == 0)
        def _():
            cnt_scr[...] = jnp.zeros(cnt_scr.shape, F32)
        cnt_scr[...] += tile_cnt

    @pl.when(ph == 1)
    def _():
        @pl.when(i == 0)
        def _():
            carry_scr[...] = jnp.zeros(carry_scr.shape, F32)
        blocks = jnp.floor((cnt_scr[...] + (MOE_BLK - 1)) * (1.0 / MOE_BLK))
        blocks_b = jnp.broadcast_to(blocks, (n_exp, LANES)).astype(BF16)
        first_blk = jnp.dot(low_ref[...], blocks_b, preferred_element_type=F32)[:, 0:1]
        prefix = jnp.dot(both.astype(BF16), tri_ref[...], preferred_element_type=F32)
        slot = first_blk * MOE_BLK + carry_scr[...] + prefix
        carry_scr[...] += tile_cnt
        pos = [jnp.sum(oh * slot, axis=0, keepdims=True).astype(I32) for oh in onehot]
        for q in range(tt // ROW_TILE):
            for k in range(TOP_K):
                pos_ref[q, :, k * ROW_TILE:(k + 1) * ROW_TILE] = pos[k][:, q * ROW_TILE:(q + 1) * ROW_TILE]
        g_ref[...] = jnp.concatenate(gates + [jnp.zeros_like(gates[0])] * (8 - TOP_K), axis=0)
        last_blk = first_blk + blocks
        bidx = lax.broadcasted_iota(I32, (n_exp, be_ref.shape[1]), 1).astype(F32)
        owner = jnp.sum((last_blk <= bidx).astype(F32), axis=0, keepdims=True)
        be_ref[...] = jnp.minimum(owner, n_exp - 1.0).astype(I32) + be_offset
        nu_ref[...] = jnp.broadcast_to(jnp.sum(blocks, axis=0, keepdims=True), nu_ref.shape).astype(I32)


def _route_plan(logits_t, router_b, be_offset):
    n_exp, t = logits_t.shape
    tt = next(m for m in (1024, 512, 256) if t % m == 0)
    n_tiles = t // tt
    nb = -(-(t * TOP_K + n_exp * (MOE_BLK - 1)) // MOE_BLK)
    nb_pad = -(-nb // LANES) * LANES
    tok_i = np.arange(tt)
    tri = jnp.asarray(tok_i[:, None] < tok_i[None, :], BF16)
    exp_i = np.arange(n_exp)
    low = jnp.asarray(exp_i[None, :] < exp_i[:, None], BF16)
    const = lambda shape: pl.BlockSpec(shape, lambda ph, i: (0,) * len(shape))
    gates, pos3, be, nu = pl.pallas_call(
        functools.partial(_route_plan_kernel, n_exp=n_exp, tt=tt, be_offset=be_offset),
        grid=(2, n_tiles),
        in_specs=[pl.BlockSpec((n_exp, tt), lambda ph, i: (0, i)),
                  const((n_exp, 1)), const((tt, tt)), const((n_exp, n_exp))],
        out_specs=[pl.BlockSpec((8, tt), lambda ph, i: (0, i * ph)),
                   pl.BlockSpec((tt // ROW_TILE, 1, TOP_K * ROW_TILE), lambda ph, i: (i * ph, 0, 0)),
                   const((1, nb_pad)), const((1, LANES))],
        out_shape=[jax.ShapeDtypeStruct((8, t), F32),
                   jax.ShapeDtypeStruct((t // ROW_TILE, 1, TOP_K * ROW_TILE), I32),
                   jax.ShapeDtypeStruct((1, nb_pad), I32),
                   jax.ShapeDtypeStruct((1, LANES), I32)],
        scratch_shapes=[pltpu.VMEM((n_exp, 1), F32), pltpu.VMEM((n_exp, 1), F32)],
        compiler_params=_cparams(("arbitrary", "arbitrary"), VMEM_LIMIT),
        name="route_plan",
    )(logits_t, router_b.reshape(n_exp, 1).astype(F32), tri, low)
    return gates, pos3, be, nu, nb


def _row_dma_loop(n_rows, make_copies):
    def body(it, carry):
        for u in range(DMA_UNROLL):
            for cp in make_copies(it * DMA_UNROLL + u):
                cp.start()
        return carry
    lax.fori_loop(0, n_rows // DMA_UNROLL, body, 0)


def _to_tile_rows(dst_ref, mat):
    m = mat.shape[0]
    for s in range(mat.shape[1] // LANES):
        dst_ref[pl.ds(s, m, stride=8), :] = mat[:, s * LANES:(s + 1) * LANES]


def _tile_row_chunk(src_ref, s, m):
    return src_ref[pl.ds(s, m, stride=8), :]


def _dispatch_kernel(pos_ref, hp_ref, xs_in_ref, xs_ref, sbuf, sem, *, n_tiles):
    del xs_in_ref
    i = pl.program_id(0)
    slot = i % 2

    def wait_slot(s):
        for _ in range(TOP_K):
            pltpu.make_async_copy(sbuf.at[s], xs_ref.at[pl.ds(0, ROW_TILE * 8)], sem.at[s]).wait()

    @pl.when(i >= 2)
    def _():
        wait_slot(slot)

    _to_tile_rows(sbuf.at[slot], hp_ref[...])

    def copies(r):
        return [pltpu.make_async_copy(
            sbuf.at[slot, pl.ds(pl.multiple_of(r * 8, 8), 8)],
            xs_ref.at[pl.ds(pl.multiple_of(pos_ref[0, 0, k * ROW_TILE + r] * 8, 8), 8)], sem.at[slot])
            for k in range(TOP_K)]

    _row_dma_loop(ROW_TILE, copies)

    @pl.when(i == n_tiles - 1)
    def _():
        wait_slot(slot)
        if n_tiles >= 2:
            wait_slot(1 - slot)


def _dispatch(hp, pos3, n_slots):
    t, half = hp.shape
    assert half == 8 * LANES, "a packed row must be exactly one (8, 128) tile"
    n_tiles = t // ROW_TILE
    xs0 = jnp.zeros((n_slots * 8, LANES), I32)
    return pl.pallas_call(
        functools.partial(_dispatch_kernel, n_tiles=n_tiles),
        grid=(n_tiles,),
        in_specs=[pl.BlockSpec((1, 1, TOP_K * ROW_TILE), lambda i: (i, 0, 0), memory_space=pltpu.SMEM),
                  pl.BlockSpec((ROW_TILE, half), lambda i: (i, 0)),
                  pl.BlockSpec(memory_space=pl.ANY)],
        out_specs=pl.BlockSpec(memory_space=pl.ANY),
        out_shape=jax.ShapeDtypeStruct((n_slots * 8, LANES), I32),
        scratch_shapes=[pltpu.VMEM((2, ROW_TILE * 8, LANES), I32), pltpu.SemaphoreType.DMA((2,))],
        input_output_aliases={2: 0},
        compiler_params=_cparams(("arbitrary",), VMEM_LIMIT),
        name="moe_dispatch",
    )(pos3, hp, xs0)


def _moe_kernel(be_ref, nu_ref, x_ref, wg_ref, wu_ref, wd_ref, y_ref):
    b = pl.program_id(0)
    half = wg_ref.shape[1] // 2
    n_used = nu_ref[0, 0]

    @pl.when(b < n_used)
    def _():
        u = jnp.concatenate([_tile_row_chunk(x_ref, s, MOE_BLK) for s in range(half // LANES)], axis=1)
        x_hi = _unpack_hi(u).astype(BF16)
        x_lo = _unpack_lo(u).astype(BF16)

        def proj(w_ref):
            return (jnp.dot(x_hi, w_ref[0, :half, :], preferred_element_type=F32)
                    + jnp.dot(x_lo, w_ref[0, half:, :], preferred_element_type=F32))

        act = (_silu(proj(wg_ref)) * proj(wu_ref)).astype(BF16)
        _to_tile_rows(y_ref, _pack_pairs(jnp.dot(act, wd_ref[0], preferred_element_type=F32)))

    @pl.when(b >= n_used)
    def _():
        y_ref[...] = jnp.zeros(y_ref.shape, y_ref.dtype)


def _moe(xs, be, n_used, wg, wu, wd):
    nb = xs.shape[0] // (MOE_BLK * 8)
    _, d, d_exp = wg.shape
    w_idx = lambda b, be, nu: (be[0, b], 0, 0)
    row_blk = pl.BlockSpec((MOE_BLK * 8, LANES), lambda b, be, nu: (b, 0))
    grid_spec = pltpu.PrefetchScalarGridSpec(
        num_scalar_prefetch=2,
        grid=(nb,),
        in_specs=[row_blk,
                  pl.BlockSpec((1, d, d_exp), w_idx),
                  pl.BlockSpec((1, d, d_exp), w_idx),
                  pl.BlockSpec((1, d_exp, d), w_idx)],
        out_specs=row_blk)
    return pl.pallas_call(
        _moe_kernel,
        grid_spec=grid_spec,
        out_shape=jax.ShapeDtypeStruct(xs.shape, I32),
        compiler_params=_cparams(("arbitrary",), VMEM_LIMIT),
        name="moe_experts",
    )(be, n_used, xs, wg, wu, wd)


def _combine_kernel(pos_ref, posn_ref, x_ref, mod_ref, gt_ref, y_hbm, o_ref, ybuf, sem, *, d, n_tiles):
    i = pl.program_id(0)
    slot = i % 2
    half = d // 2

    def issue(p_ref, s):
        def copies(r):
            return [pltpu.make_async_copy(
                y_hbm.at[pl.ds(pl.multiple_of(p_ref[0, 0, k * ROW_TILE + r] * 8, 8), 8)],
                ybuf.at[s, k, pl.ds(pl.multiple_of(r * 8, 8), 8)], sem.at[s])
                for k in range(TOP_K)]
        _row_dma_loop(ROW_TILE, copies)

    @pl.when(i == 0)
    def _():
        issue(pos_ref, 0)

    @pl.when(i + 1 < n_tiles)
    def _():
        issue(posn_ref, 1 - slot)

    for k in range(TOP_K):
        pltpu.make_async_copy(y_hbm.at[pl.ds(0, ROW_TILE * 8)], ybuf.at[slot, k], sem.at[slot]).wait()

    w0, w1 = gt_ref[:, 0:1], gt_ref[:, 1:2]
    for s in range(half // LANES):
        u0 = _tile_row_chunk(ybuf.at[slot, 0], s, ROW_TILE)
        u1 = _tile_row_chunk(ybuf.at[slot, 1], s, ROW_TILE)
        for off, unpack in ((0, _unpack_hi), (half, _unpack_lo)):
            lo, hi = off + s * LANES, off + (s + 1) * LANES
            f = w0 * unpack(u0) + w1 * unpack(u1)
            o_ref[:, lo:hi] = x_ref[:, lo:hi] + mod_ref[0, :, 5 * d + lo:5 * d + hi] * f


def _combine(x_mid, mod3, layer, gates, pos3, y, *, n_x_tiles, tiles_per_seq):
    t, d = x_mid.shape
    n_batch = n_x_tiles // tiles_per_seq
    n_tiles = t // ROW_TILE
    half = d // 2

    def mod_idx(i):
        return (layer * 8 + jnp.where(i < n_x_tiles, i // tiles_per_seq, n_batch), 0, 0)

    smem_blk = lambda f: pl.BlockSpec((1, 1, TOP_K * ROW_TILE), f, memory_space=pltpu.SMEM)
    return pl.pallas_call(
        functools.partial(_combine_kernel, d=d, n_tiles=n_tiles),
        grid=(n_tiles,),
        in_specs=[smem_blk(lambda i: (i, 0, 0)),
                  smem_blk(lambda i: (jnp.minimum(i + 1, n_tiles - 1), 0, 0)),
                  pl.BlockSpec((ROW_TILE, d), lambda i: (i, 0)),
                  pl.BlockSpec((1, 1, mod3.shape[2]), mod_idx),
                  pl.BlockSpec((ROW_TILE, TOP_K), lambda i: (i, 0)),
                  pl.BlockSpec(memory_space=pl.ANY)],
        out_specs=pl.BlockSpec((ROW_TILE, d), lambda i: (i, 0)),
        out_shape=jax.ShapeDtypeStruct((t, d), F32),
        scratch_shapes=[pltpu.VMEM((2, TOP_K, ROW_TILE * 8, LANES), I32), pltpu.SemaphoreType.DMA((2,))],
        compiler_params=_cparams(("arbitrary",), VMEM_LIMIT),
        name="moe_combine",
    )(pos3, pos3, x_mid, mod3, gates, y)


def _rope_tables(seq, tm):
    nf = HEAD_DIM // 4
    inv = jnp.power(ROPE_BASE, -jnp.arange(nf, dtype=F32) / nf)
    tt = jnp.arange(seq)
    ar = (tt // GRID_W).astype(F32)[:, None] * inv
    ac = (tt % GRID_W).astype(F32)[:, None] * inv
    ang = jnp.concatenate([ar, ar, ac, ac], axis=-1)
    cos = jnp.concatenate([jnp.cos(ang), jnp.ones((tm, HEAD_DIM), F32)], axis=0)
    sin = jnp.concatenate([jnp.sin(ang), jnp.zeros((tm, HEAD_DIM), F32)], axis=0)
    return cos, sin


def kernel(x, c, ctx, c_ctx, ada_w, ada_b, norm_mix_g, norm_ffn_g, ab_w_in, ab_w_out, ab_q_norm, ab_k_norm, ab_sink, conv_w, conv_b, conv_ln_g, conv_ln_b, na_w_in, na_w_out, na_q_norm, na_k_norm, na_rpb, router_w, router_b, moe_w_gate, moe_w_up, moe_w_down):
    batch, seq, d = x.shape
    ctx_len = ctx.shape[1]
    depth = ada_w.shape[0]
    n_exp = router_w.shape[1]
    b_ch = conv_w.shape[-1]
    a_qw = ab_w_out.shape[1] - b_ch
    a_kvw = (ab_w_in.shape[-1] - a_qw - 2 * b_ch) // 2
    n_kv = 2
    na_w = na_w_out.shape[1]
    assert batch + 1 <= 8 and a_qw == b_ch and n_exp % 8 == 0

    tx, tc = batch * seq, batch * ctx_len
    t = tx + tc
    tm_in = next(m for m in (1024, 512, 256) if seq % m == 0 and tc % m == 0)
    tn = 512 if d >= 2048 else 256

    c8 = jnp.zeros((8, d), F32).at[:batch].set(c).at[batch].set(c_ctx)
    mod3 = _ada_all(c8, ada_w, ada_b).reshape(depth * 8, 1, 6 * d)

    cos, sin = _rope_tables(seq, tm_in)
    rw = jnp.zeros((d, LANES), F32).at[:, :n_exp].set(router_w.astype(F32))
    rw_hi = rw.astype(BF16)
    rw_cat = jnp.concatenate([rw_hi, (rw - rw_hi.astype(F32)).astype(BF16)], axis=1)
    rows = seq // GRID_W

    k0, v0, u0 = a_qw, a_qw + a_kvw, a_qw + 2 * a_kvw
    k_off, v_off = a_qw + 2 * b_ch, a_qw + 2 * b_ch + a_kvw
    ab_w_in_b = jnp.concatenate([ab_w_in[..., :k0], ab_w_in[..., u0:], ab_w_in[..., k0:u0]], axis=-1).astype(BF16)
    ab_w_out_b = ab_w_out.astype(BF16)
    na_w_in_b = na_w_in.astype(BF16)
    na_w_out_b = na_w_out.astype(BF16)
    d_exp = moe_w_gate.shape[-1]
    wg_all = moe_w_gate.astype(BF16).reshape(depth * n_exp, d, d_exp)
    wu_all = moe_w_up.astype(BF16).reshape(depth * n_exp, d, d_exp)
    wd_all = moe_w_down.astype(BF16).reshape(depth * n_exp, d_exp, d)

    def ab_kind(col):
        return "q" if col < a_qw else ("k" if k_off <= col < v_off else "p")

    def na_kind(col):
        return "q" if col < na_w else ("k" if col < 2 * na_w else "p")

    tok = jnp.concatenate([x.reshape(tx, d), ctx.reshape(tc, d)], axis=0)

    for i in range(depth):
        with_ctx = i < depth - 1
        j = i // 2
        t_act = t if with_ctx else tx
        tiles = dict(n_x_tiles=tx // tm_in, tiles_per_seq=seq // tm_in)
        if i % 2 == 0:
            px = _inproj(tok, mod3, i, norm_mix_g[i], ab_w_in_b, j, ab_q_norm[j], ab_k_norm[j], ab_kind, cos, sin,
                         tm=tm_in, tn=tn, **tiles)
            att = _win_attn(px, ab_sink[j], batch=batch, seq=seq, ctx_len=ctx_len, a_qw=a_qw, n_kv=n_kv,
                            k_off=k_off, v_off=v_off, with_ctx=with_ctx)
            cv = _conv(px, conv_w[j], conv_b[j], conv_ln_g[j], conv_ln_b[j], a_off=a_qw, g_off=a_qw + b_ch,
                       batch=batch, seq=seq, ctx_len=ctx_len, with_ctx=with_ctx, tmc=256)
            acts, w_out = [att, cv], ab_w_out_b
        else:
            px = _inproj(tok, mod3, i, norm_mix_g[i], na_w_in_b, j, na_q_norm[j], na_k_norm[j], na_kind, None, None,
                         tm=tm_in, tn=tn, **tiles)
            bias = _na_bias_table(na_rpb[j], rows)
            att = _na_attn(px, bias, batch=batch, seq=seq, ctx_len=ctx_len, n_heads=na_w // HEAD_DIM,
                           with_ctx=with_ctx)
            acts, w_out = [att], na_w_out_b

        otiles = dict(n_x_tiles=tx // ROW_TILE, tiles_per_seq=seq // ROW_TILE)
        x_mid, hp, logits_t = _outproj(tok, mod3, i, norm_ffn_g[i], rw_cat, rw_hi, acts, w_out, j, t_act=t_act,
                                       n_exp=n_exp, tm=ROW_TILE, **otiles)
        g_out, pos3, be, n_used, nb = _route_plan(logits_t, router_b, i * n_exp)
        xs = _dispatch(hp, pos3, nb * MOE_BLK)
        y = _moe(xs, be, n_used, wg_all, wu_all, wd_all)
        tok = _combine(x_mid, mod3, i, g_out[:TOP_K].T, pos3, y, **otiles)

    return tok[:tx].reshape(batch, seq, d)
```

```python
import functools

import numpy as np
import jax
import jax.numpy as jnp
from jax import lax
from jax.experimental import pallas as pl
from jax.experimental.pallas import tpu as pltpu

F32 = jnp.float32
BF16 = jnp.bfloat16
I32 = jnp.int32

HEAD_DIM = 128
LANES = 128
GRID_W = 64
WINDOW = 128
N_GROUPS = 4
TOP_K = 2
ROPE_BASE = 10000.0
EPS = 1e-6
NEG_INF = -1e30
MOE_BLK = 256
ROW_TILE = 256
NA_RB = 4
NA_HG = 4
HALO = 16
DMA_UNROLL = 64
VMEM_LIMIT = 56 * 1024 * 1024
HI_MASK = -65536
LOG2E = 1.4426950408889634
Q_SCALE = HEAD_DIM ** -0.5 * LOG2E


def _cparams(sem, vmem=None):
    return pltpu.CompilerParams(dimension_semantics=sem, vmem_limit_bytes=vmem)


def _silu(v):
    return v * jax.nn.sigmoid(v)


def _rms(v, g):
    ms = jnp.mean(v * v, axis=-1, keepdims=True)
    return v * lax.rsqrt(ms + EPS) * g


def _pack_pairs(v):
    half = v.shape[1] // 2
    bits = pltpu.bitcast(v.astype(BF16).astype(F32), I32)
    return (bits[:, :half] & HI_MASK) | lax.shift_right_logical(bits[:, half:], 16)


def _unpack_hi(u):
    return pltpu.bitcast(u & HI_MASK, F32)


def _unpack_lo(u):
    return pltpu.bitcast(lax.shift_left(u, 16), F32)


def _ada_kernel(c_ref, w_ref, b_ref, o_ref):
    sc = _silu(c_ref[...])
    o_ref[0] = jnp.dot(sc.astype(BF16), w_ref[0].astype(BF16), preferred_element_type=F32) + b_ref[0]


def _ada_all(c8, ada_w, ada_b):
    depth, d, n = ada_w.shape
    tn = min(n, 1024)
    return pl.pallas_call(
        _ada_kernel,
        grid=(depth, n // tn),
        in_specs=[pl.BlockSpec((8, d), lambda l, j: (0, 0)),
                  pl.BlockSpec((1, d, tn), lambda l, j: (l, 0, j)),
                  pl.BlockSpec((1, 1, tn), lambda l, j: (l, 0, j))],
        out_specs=pl.BlockSpec((1, 8, tn), lambda l, j: (l, 0, j)),
        out_shape=jax.ShapeDtypeStruct((depth, 8, n), F32),
        compiler_params=_cparams(("arbitrary", "arbitrary"), VMEM_LIMIT),
        name="ada_mod",
    )(c8, ada_w, ada_b.reshape(depth, 1, n))


def _rope(y, cos, sin):
    lane = lax.broadcasted_iota(I32, y.shape, 1)
    first = (lane & 32) == 0
    fwd = pltpu.roll(y, 32, 1)
    bwd = pltpu.roll(y, 96, 1)
    return y * cos + jnp.where(first, -bwd, fwd) * sin


def _inproj_kernel(*refs, d, tn, groups, rope):
    if rope:
        x_ref, mod_ref, g_ref, w_ref, qg_ref, kg_ref, cos_ref, sin_ref, o_ref, h_scr = refs
    else:
        x_ref, mod_ref, g_ref, w_ref, qg_ref, kg_ref, o_ref, h_scr = refs
        cos_ref = sin_ref = None
    j = pl.program_id(1)

    @pl.when(j == 0)
    def _():
        y = _rms(x_ref[...], g_ref[...])
        shift = mod_ref[0, :, 0:d]
        scale = mod_ref[0, :, d:2 * d]
        h_scr[...] = (y * (1.0 + scale) + shift).astype(BF16)

    acc = jnp.dot(h_scr[...], w_ref[...], preferred_element_type=F32)

    for kinds, lo, hi in groups:
        @pl.when((j >= lo) & (j <= hi))
        def _(kinds=kinds):
            if all(k == "p" for k in kinds):
                o_ref[...] = acc.astype(o_ref.dtype)
                return
            for s, kind in enumerate(kinds):
                piece = acc[:, s * LANES:(s + 1) * LANES]
                if kind != "p":
                    piece = _rms(piece, (qg_ref if kind == "q" else kg_ref)[...])
                    if rope:
                        piece = _rope(piece, cos_ref[...], sin_ref[...])
                    if kind == "q":
                        piece = piece * Q_SCALE
                o_ref[:, s * LANES:(s + 1) * LANES] = piece.astype(o_ref.dtype)


def _inproj(tok, mod3, layer, norm_g, w, w_layer, qg, kg, kind_of_col, cos, sin, *, n_x_tiles, tiles_per_seq, tm, tn):
    t, d = tok.shape
    n = w.shape[2]
    nj = n // tn
    per_j = [tuple(kind_of_col(j * tn + s * LANES) for s in range(tn // LANES)) for j in range(nj)]
    groups = []
    for j, kinds in enumerate(per_j):
        if groups and groups[-1][0] == kinds and groups[-1][2] == j - 1:
            groups[-1] = (kinds, groups[-1][1], j)
        else:
            groups.append((kinds, j, j))
    rope = cos is not None
    n_batch = n_x_tiles // tiles_per_seq

    def mod_idx(i, j):
        return (layer * 8 + jnp.where(i < n_x_tiles, i // tiles_per_seq, n_batch), 0, 0)

    def pos_idx(i, j):
        return (jnp.where(i < n_x_tiles, i % tiles_per_seq, tiles_per_seq), 0)

    in_specs = [pl.BlockSpec((tm, d), lambda i, j: (i, 0)),
                pl.BlockSpec((1, 1, mod3.shape[2]), mod_idx),
                pl.BlockSpec((1, d), lambda i, j: (0, 0)),
                pl.BlockSpec((None, d, tn), lambda i, j: (w_layer, 0, j)),
                pl.BlockSpec((1, HEAD_DIM), lambda i, j: (0, 0)),
                pl.BlockSpec((1, HEAD_DIM), lambda i, j: (0, 0))]
    args = [tok, mod3, norm_g.reshape(1, d), w, qg.reshape(1, HEAD_DIM), kg.reshape(1, HEAD_DIM)]
    if rope:
        in_specs += [pl.BlockSpec((tm, HEAD_DIM), pos_idx), pl.BlockSpec((tm, HEAD_DIM), pos_idx)]
        args += [cos, sin]
    return pl.pallas_call(
        functools.partial(_inproj_kernel, d=d, tn=tn, groups=tuple(groups), rope=rope),
        grid=(t // tm, nj),
        in_specs=in_specs,
        out_specs=pl.BlockSpec((tm, tn), lambda i, j: (i, j)),
        out_shape=jax.ShapeDtypeStruct((t, n), BF16),
        scratch_shapes=[pltpu.VMEM((tm, d), BF16)],
        compiler_params=_cparams(("arbitrary", "arbitrary"), VMEM_LIMIT),
        name="in_proj",
    )(*args)


def _softmax_pv(s, v, extra_logit=None):
    m = jnp.max(s, axis=-1, keepdims=True)
    if extra_logit is not None:
        m = jnp.maximum(m, extra_logit)
    p = jnp.exp2(s - m)
    den = jnp.sum(p, axis=-1, keepdims=True)
    if extra_logit is not None:
        den = den + jnp.exp2(extra_logit - m)
    o = jnp.dot(p.astype(BF16), v, preferred_element_type=F32)
    return o / den


def _win_attn_kernel(sink_ref, q_ref, kp_ref, kc_ref, kn_ref, vp_ref, vc_ref, vn_ref, kx_ref, vx_ref, o_ref,
                     *, n_grp, nb):
    h = pl.program_id(1)
    n = pl.program_id(2)
    w = WINDOW
    q = q_ref[...]
    qs = jnp.concatenate([q[:, g * HEAD_DIM:(g + 1) * HEAD_DIM] for g in range(n_grp)], axis=0)
    k = jnp.concatenate([kp_ref[...], kc_ref[...], kn_ref[...], kx_ref[...]], axis=0)
    v = jnp.concatenate([vp_ref[...], vc_ref[...], vn_ref[...], vx_ref[...]], axis=0)
    s = lax.dot_general(qs, k, (((1,), (1,)), ((), ())), preferred_element_type=F32)
    rows = lax.broadcasted_iota(I32, s.shape, 0) & (w - 1)
    cols = lax.broadcasted_iota(I32, s.shape, 1)
    is_x = n < nb
    lo = jnp.where(is_x, jnp.where(n > 0, 0, w), 0)
    hi = jnp.where(is_x, jnp.where(n < nb - 1, 3 * w, 2 * w), 0)
    local_ok = (jnp.abs(cols - w - rows) <= WINDOW) & (cols >= lo) & (cols < hi)
    s = jnp.where(local_ok | (cols >= 3 * w), s, NEG_INF)
    for g in range(n_grp):
        o = _softmax_pv(s[g * w:(g + 1) * w], v, sink_ref[h, g] * LOG2E)
        o_ref[:, g * HEAD_DIM:(g + 1) * HEAD_DIM] = o.astype(o_ref.dtype)


def _win_attn(px, sink, *, batch, seq, ctx_len, a_qw, n_kv, k_off, v_off, with_ctx):
    t = px.shape[0]
    w = WINDOW
    n_grp = a_qw // HEAD_DIM // n_kv
    nb = seq // w
    nq = nb + (ctx_len // w if with_ctx else 0)
    qw = n_grp * HEAD_DIM
    ctx_blk0 = batch * seq // ctx_len

    def q_idx(b, h, n):
        return (jnp.where(n < nb, b * nb + n, batch * nb + b * (ctx_len // w) + (n - nb)), h)

    def kv_idx(off, delta):
        def f(b, h, n):
            return (b * nb + jnp.clip(n + delta, 0, nb - 1), off // HEAD_DIM + h)
        return f

    def ctx_idx(off):
        return lambda b, h, n: (ctx_blk0 + b, off // HEAD_DIM + h)

    blk = lambda f: pl.BlockSpec((w, HEAD_DIM), f)
    in_specs = [pl.BlockSpec(memory_space=pltpu.SMEM),
                pl.BlockSpec((w, qw), q_idx),
                blk(kv_idx(k_off, -1)), blk(kv_idx(k_off, 0)), blk(kv_idx(k_off, 1)),
                blk(kv_idx(v_off, -1)), blk(kv_idx(v_off, 0)), blk(kv_idx(v_off, 1)),
                pl.BlockSpec((ctx_len, HEAD_DIM), ctx_idx(k_off)),
                pl.BlockSpec((ctx_len, HEAD_DIM), ctx_idx(v_off))]
    return pl.pallas_call(
        functools.partial(_win_attn_kernel, n_grp=n_grp, nb=nb),
        grid=(batch, n_kv, nq),
        in_specs=in_specs,
        out_specs=pl.BlockSpec((w, qw), q_idx),
        out_shape=jax.ShapeDtypeStruct((t if with_ctx else batch * seq, a_qw), BF16),
        compiler_params=_cparams(("arbitrary",) * 3, VMEM_LIMIT),
        name="win_attn",
    )(sink.reshape(n_kv, n_grp).astype(F32), *([px] * 9))


def _na_attn_kernel(q_ref, kp_ref, kc_ref, kn_ref, vp_ref, vc_ref, vn_ref, kx_ref, vx_ref, bias_ref, o_ref,
                    *, n_hg):
    nloc = 3 * NA_RB * GRID_W
    for hh in range(n_hg):
        cl = slice(hh * HEAD_DIM, (hh + 1) * HEAD_DIM)
        k = jnp.concatenate([kp_ref[:, cl], kc_ref[:, cl], kn_ref[:, cl], kx_ref[:, cl]], axis=0)
        v = jnp.concatenate([vp_ref[:, cl], vc_ref[:, cl], vn_ref[:, cl], vx_ref[:, cl]], axis=0)
        s = lax.dot_general(q_ref[:, cl], k, (((1,), (1,)), ((), ())), preferred_element_type=F32)
        s = jnp.concatenate([s[:, :nloc] + bias_ref[hh, 0], s[:, nloc:]], axis=1)
        o_ref[:, cl] = _softmax_pv(s, v).astype(o_ref.dtype)


def _na_bias_table(rpb, rows):
    n_heads, n_dr, n_dc = rpb.shape
    kh, kw = (n_dr + 1) // 2, (n_dc + 1) // 2
    n_rb = rows // NA_RB
    cidx = np.arange(GRID_W)
    cs = np.clip(cidx - kw // 2, 0, GRID_W - kw)
    col_ok = (cidx[None, :] >= cs[:, None]) & (cidx[None, :] < cs[:, None] + kw)
    dc_idx = np.clip(cidx[None, :] - cidx[:, None], -(kw - 1), kw - 1) + kw - 1
    a = jnp.where(col_ok[None, None], rpb.astype(F32)[:, :, dc_idx] * LOG2E, NEG_INF)
    masked = jnp.full((n_heads, GRID_W, GRID_W), NEG_INF, F32)
    classes = []
    for rb in (0, min(1, n_rb - 1), n_rb - 1):
        qrows = []
        for j in range(NA_RB):
            r = rb * NA_RB + j
            rs = int(np.clip(r - kh // 2, 0, rows - kh))
            blocks = []
            for tblk in range(3):
                for krl in range(NA_RB):
                    kr = (rb - 1 + tblk) * NA_RB + krl
                    ok = (rs <= kr < rs + kh) and (0 <= kr < rows)
                    blocks.append(a[:, kr - r + kh - 1] if ok else masked)
            qrows.append(jnp.concatenate(blocks, axis=-1))
        classes.append(jnp.concatenate(qrows, axis=1))
    classes.append(jnp.full_like(classes[0], NEG_INF))
    return jnp.stack(classes, axis=1)


def _na_attn(px, bias, *, batch, seq, ctx_len, n_heads, with_ctx):
    t = px.shape[0]
    qb = NA_RB * GRID_W
    assert ctx_len == qb, "context queries are processed as one extra query block"
    n_rb = seq // qb
    nq = n_rb + (1 if with_ctx else 0)
    na_w = n_heads * HEAD_DIM
    n_hg = min(NA_HG, n_heads)
    gw = n_hg * HEAD_DIM
    assert n_heads % n_hg == 0
    ctx_blk0 = batch * seq // ctx_len

    def q_idx(b, h, r):
        return (jnp.where(r < n_rb, b * n_rb + r, batch * n_rb + b), h)

    def kv_idx(off, delta):
        return lambda b, h, r: (b * n_rb + jnp.clip(r + delta, 0, n_rb - 1), off // gw + h)

    def ctx_idx(off):
        return lambda b, h, r: (ctx_blk0 + b, off // gw + h)

    def bias_idx(b, h, r):
        return (h, jnp.where(r == 0, 0, jnp.where(r < n_rb - 1, 1, jnp.where(r == n_rb - 1, 2, 3))), 0, 0)

    blk = lambda f: pl.BlockSpec((qb, gw), f)
    in_specs = [blk(q_idx),
                blk(kv_idx(na_w, -1)), blk(kv_idx(na_w, 0)), blk(kv_idx(na_w, 1)),
                blk(kv_idx(2 * na_w, -1)), blk(kv_idx(2 * na_w, 0)), blk(kv_idx(2 * na_w, 1)),
                pl.BlockSpec((ctx_len, gw), ctx_idx(na_w)),
                pl.BlockSpec((ctx_len, gw), ctx_idx(2 * na_w)),
                pl.BlockSpec((n_hg, 1, qb, 3 * qb), bias_idx)]
    return pl.pallas_call(
        functools.partial(_na_attn_kernel, n_hg=n_hg),
        grid=(batch, n_heads // n_hg, nq),
        in_specs=in_specs,
        out_specs=blk(q_idx),
        out_shape=jax.ShapeDtypeStruct((t if with_ctx else batch * seq, na_w), BF16),
        compiler_params=_cparams(("arbitrary",) * 3, VMEM_LIMIT),
        name="na_attn",
    )(*([px] * 9), bias)


def _conv_kernel(a_ref, g_ref, ap_ref, gp_ref, an_ref, gn_ref, w_ref, b_ref, lg_ref, lb_ref, o_ref, hbuf, hs, cbuf,
                 *, tiles_per_seq, n_x_tiles, n_taps, tmc, sub):
    i = pl.program_id(0)
    p = i % tiles_per_seq
    is_x = i < n_x_tiles
    has_prev = is_x & (p > 0)
    has_next = is_x & (p < tiles_per_seq - 1)

    def glu(a, g):
        return a.astype(F32) * jax.nn.sigmoid(g.astype(F32))

    hbuf[0:HALO, :] = jnp.where(has_prev, glu(ap_ref[...], gp_ref[...]), 0.0)
    hbuf[HALO:HALO + tmc, :] = glu(a_ref[...], g_ref[...])
    hbuf[HALO + tmc:2 * HALO + tmc, :] = jnp.where(has_next, glu(an_ref[...], gn_ref[...]), 0.0)

    n_buf = tmc + 2 * HALO
    for sh in range(1, 8):
        hs[sh - 1, 0:n_buf - 8, :] = hbuf[sh:sh + n_buf - 8, :]

    ch = a_ref.shape[1]
    first = HALO - n_taps // 2
    for c in range(ch // LANES):
        cl = slice(c * LANES, (c + 1) * LANES)
        for tb in range(tmc // sub):
            acc = jnp.zeros((sub, LANES), F32)
            for k in range(n_taps):
                sh = (first + k) % 8
                r0 = tb * sub + first + k - sh
                slab = hbuf[r0:r0 + sub, cl] if sh == 0 else hs[sh - 1, r0:r0 + sub, cl]
                acc = acc + slab * w_ref[k:k + 1, cl]
            cbuf[tb * sub:(tb + 1) * sub, cl] = acc + b_ref[:, cl]

    y = cbuf[...]
    mu = jnp.mean(y, axis=-1, keepdims=True)
    yc = y - mu
    var = jnp.mean(yc * yc, axis=-1, keepdims=True)
    yn = yc * lax.rsqrt(var + EPS) * lg_ref[...] + lb_ref[...]
    o_ref[...] = _silu(yn).astype(o_ref.dtype)


def _conv(px, conv_w, conv_b, ln_g, ln_b, *, a_off, g_off, batch, seq, ctx_len, with_ctx, tmc):
    t = px.shape[0]
    n_taps, ch = conv_w.shape
    assert n_taps // 2 <= HALO and ctx_len == tmc and seq % tmc == 0
    tiles_per_seq = seq // tmc
    n_x_tiles = batch * tiles_per_seq
    n_tiles = n_x_tiles + (batch if with_ctx else 0)
    hpt = tmc // HALO
    n_hblk = t // HALO
    w_pad = jnp.zeros((32, ch), F32).at[:n_taps].set(conv_w.astype(F32))

    main = lambda off: pl.BlockSpec((tmc, ch), lambda i: (i, off // ch))
    prev = lambda off: pl.BlockSpec((HALO, ch), lambda i: (jnp.maximum(i * hpt - 1, 0), off // ch))
    nxt = lambda off: pl.BlockSpec((HALO, ch), lambda i: (jnp.minimum((i + 1) * hpt, n_hblk - 1), off // ch))
    vec = lambda: pl.BlockSpec((1, ch), lambda i: (0, 0))
    return pl.pallas_call(
        functools.partial(_conv_kernel, tiles_per_seq=tiles_per_seq, n_x_tiles=n_x_tiles, n_taps=n_taps,
                          tmc=tmc, sub=64),
        grid=(n_tiles,),
        in_specs=[main(a_off), main(g_off), prev(a_off), prev(g_off), nxt(a_off), nxt(g_off),
                  pl.BlockSpec((32, ch), lambda i: (0, 0)), vec(), vec(), vec()],
        out_specs=pl.BlockSpec((tmc, ch), lambda i: (i, 0)),
        out_shape=jax.ShapeDtypeStruct((t if with_ctx else batch * seq, ch), BF16),
        scratch_shapes=[pltpu.VMEM((tmc + 2 * HALO, ch), F32), pltpu.VMEM((7, tmc + 2 * HALO, ch), F32),
                        pltpu.VMEM((tmc, ch), F32)],
        compiler_params=_cparams(("arbitrary",), VMEM_LIMIT),
        name="conformer_conv",
    )(px, px, px, px, px, px, w_pad, conv_b.reshape(1, ch).astype(F32), ln_g.reshape(1, ch).astype(F32),
      ln_b.reshape(1, ch).astype(F32))


def _outproj_kernel(*refs, d, n_in, n_exp):
    x_ref, mod_ref, g_ref, rwc_ref, rwh_ref = refs[:5]
    a_refs = refs[5:5 + n_in]
    w_refs = refs[5 + n_in:5 + 2 * n_in]
    xo_ref, hp_ref, lt_ref = refs[5 + 2 * n_in:]
    o = jnp.dot(a_refs[0][...], w_refs[0][...], preferred_element_type=F32)
    for a_ref, w_ref in zip(a_refs[1:], w_refs[1:]):
        o = o + jnp.dot(a_ref[...], w_ref[...], preferred_element_type=F32)
    x_new = x_ref[...] + mod_ref[0, :, 2 * d:3 * d] * o
    xo_ref[...] = x_new
    hf = _rms(x_new, g_ref[...]) * (1.0 + mod_ref[0, :, 4 * d:5 * d]) + mod_ref[0, :, 3 * d:4 * d]
    hp_ref[...] = _pack_pairs(hf)
    h_hi = hf.astype(BF16)
    h_lo = (hf - h_hi.astype(F32)).astype(BF16)
    both = jnp.dot(h_hi, rwc_ref[...], preferred_element_type=F32)
    lg = both[:, :LANES] + both[:, LANES:] + jnp.dot(h_lo, rwh_ref[...], preferred_element_type=F32)
    lt_ref[...] = lg.T[:n_exp]


def _outproj(tok, mod3, layer, norm_g, rw_cat, rw_hi, acts, w, w_layer, *, t_act, n_exp, n_x_tiles, tiles_per_seq, tm):
    d = tok.shape[1]
    n_batch = n_x_tiles // tiles_per_seq
    n_in = len(acts)

    def mod_idx(i):
        return (layer * 8 + jnp.where(i < n_x_tiles, i // tiles_per_seq, n_batch), 0, 0)

    in_specs = [pl.BlockSpec((tm, d), lambda i: (i, 0)),
                pl.BlockSpec((1, 1, mod3.shape[2]), mod_idx),
                pl.BlockSpec((1, d), lambda i: (0, 0)),
                pl.BlockSpec(rw_cat.shape, lambda i: (0, 0)),
                pl.BlockSpec(rw_hi.shape, lambda i: (0, 0))]
    in_specs += [pl.BlockSpec((tm, a.shape[1]), lambda i: (i, 0)) for a in acts]
    kw = acts[0].shape[1]
    assert all(a.shape[1] == kw for a in acts) and w.shape[1] == kw * n_in
    in_specs += [pl.BlockSpec((None, kw, d), lambda i, r=r: (w_layer, r, 0)) for r in range(n_in)]
    return pl.pallas_call(
        functools.partial(_outproj_kernel, d=d, n_in=n_in, n_exp=n_exp),
        grid=(t_act // tm,),
        in_specs=in_specs,
        out_specs=[pl.BlockSpec((tm, d), lambda i: (i, 0)),
                   pl.BlockSpec((tm, d // 2), lambda i: (i, 0)),
                   pl.BlockSpec((n_exp, tm), lambda i: (0, i))],
        out_shape=[jax.ShapeDtypeStruct((t_act, d), F32),
                   jax.ShapeDtypeStruct((t_act, d // 2), I32),
                   jax.ShapeDtypeStruct((n_exp, t_act), F32)],
        compiler_params=_cparams(("arbitrary",), VMEM_LIMIT),
        name="out_proj",
    )(tok, mod3, norm_g.reshape(1, d), rw_cat, rw_hi, *acts, *([w] * n_in))


def _route_tile(logits, bias, n_exp):
    epg = n_exp // N_GROUPS
    aff = jax.nn.sigmoid(logits)
    sel = aff + bias
    row = lambda a, i: a[i:i + 1, :]

    scores = []
    for g in range(N_GROUPS):
        best = None
        for i in range(epg):
            for j in range(i + 1, epg):
                pair = row(sel, g * epg + i) + row(sel, g * epg + j)
                best = pair if best is None else jnp.maximum(best, pair)
        scores.append(best)
    grp = jnp.zeros(scores[0].shape, I32)
    top = scores[0]
    for g in range(1, N_GROUPS):
        better = scores[g] > top
        grp = jnp.where(better, g, grp)
        top = jnp.where(better, scores[g], top)

    def pick(a, i):
        out = row(a, i)
        for g in range(1, N_GROUPS):
            out = jnp.where(grp == g, row(a, g * epg + i), out)
        return out

    v = [pick(sel, i) for i in range(epg)]
    a = [pick(aff, i) for i in range(epg)]
    ranks = []
    for i in range(epg):
        r = jnp.zeros(grp.shape, I32)
        for j in range(epg):
            if j != i:
                ahead = (v[j] > v[i]) | ((v[j] == v[i]) & (j < i)) if j < i else (v[j] > v[i])
                r = r + ahead.astype(I32)
        ranks.append(r)
    zero_i, zero_f = jnp.zeros(grp.shape, I32), jnp.zeros(grp.shape, F32)
    experts, affs = [], []
    for k in range(TOP_K):
        e_k, a_k = zero_i, zero_f
        for i in range(epg):
            hit = ranks[i] == k
            e_k = jnp.where(hit, i, e_k)
            a_k = jnp.where(hit, a[i], a_k)
        experts.append(grp * epg + e_k)
        affs.append(a_k)
    den = affs[0] + affs[1]
    return experts, [a_k / den for a_k in affs]


def _route_plan_kernel(l_ref, b_ref, tri_ref, low_ref, g_ref, pos_ref, be_ref, nu_ref, cnt_scr, carry_scr,
                       *, n_exp, tt, be_offset):
    ph = pl.program_id(0)
    i = pl.program_id(1)
    experts, gates = _route_tile(l_ref[...], b_ref[...], n_exp)
    eid = lax.broadcasted_iota(I32, (n_exp, tt), 0)
    onehot = [(eid == e_k).astype(F32) for e_k in experts]
    both = onehot[0] + onehot[1]
    tile_cnt = jnp.sum(both, axis=1, keepdims=True)

    @pl.when(ph == 0)
    def _():
        @pl.when(i == 0)
        def _():
            cnt_scr[...] = jnp.zeros(cnt_scr.shape, F32)
        cnt_scr[...] += tile_cnt

    @pl.when(ph == 1)
    def _():
        @pl.when(i == 0)
        def _():
            carry_scr[...] = jnp.zeros(carry_scr.shape, F32)
        blocks = jnp.floor((cnt_scr[...] + (MOE_BLK - 1)) * (1.0 / MOE_BLK))
        blocks_b = jnp.broadcast_to(blocks, (n_exp, LANES)).astype(BF16)
        first_blk = jnp.dot(low_ref[...], blocks_b, preferred_element_type=F32)[:, 0:1]
        prefix = jnp.dot(both.astype(BF16), tri_ref[...], preferred_element_type=F32)
        slot = first_blk * MOE_BLK + carry_scr[...] + prefix
        carry_scr[...] += tile_cnt
        pos = [jnp.sum(oh * slot, axis=0, keepdims=True).astype(I32) for oh in onehot]
        for q in range(tt // ROW_TILE):
            for k in range(TOP_K):
                pos_ref[q, :, k * ROW_TILE:(k + 1) * ROW_TILE] = pos[k][:, q * ROW_TILE:(q + 1) * ROW_TILE]
        g_ref[...] = jnp.concatenate(gates + [jnp.zeros_like(gates[0])] * (8 - TOP_K), axis=0)
        last_blk = first_blk + blocks
        bidx = lax.broadcasted_iota(I32, (n_exp, be_ref.shape[1]), 1).astype(F32)
        owner = jnp.sum((last_blk <= bidx).astype(F32), axis=0, keepdims=True)
        be_ref[...] = jnp.minimum(owner, n_exp - 1.0).astype(I32) + be_offset
        nu_ref[...] = jnp.broadcast_to(jnp.sum(blocks, axis=0, keepdims=True), nu_ref.shape).astype(I32)


def _route_plan(logits_t, router_b, be_offset):
    n_exp, t = logits_t.shape
    tt = next(m for m in (1024, 512, 256) if t % m == 0)
    n_tiles = t // tt
    nb = -(-(t * TOP_K + n_exp * (MOE_BLK - 1)) // MOE_BLK)
    nb_pad = -(-nb // LANES) * LANES
    tok_i = np.arange(tt)
    tri = jnp.asarray(tok_i[:, None] < tok_i[None, :], BF16)
    exp_i = np.arange(n_exp)
    low = jnp.asarray(exp_i[None, :] < exp_i[:, None], BF16)
    const = lambda shape: pl.BlockSpec(shape, lambda ph, i: (0,) * len(shape))
    gates, pos3, be, nu = pl.pallas_call(
        functools.partial(_route_plan_kernel, n_exp=n_exp, tt=tt, be_offset=be_offset),
        grid=(2, n_tiles),
        in_specs=[pl.BlockSpec((n_exp, tt), lambda ph, i: (0, i)),
                  const((n_exp, 1)), const((tt, tt)), const((n_exp, n_exp))],
        out_specs=[pl.BlockSpec((8, tt), lambda ph, i: (0, i * ph)),
                   pl.BlockSpec((tt // ROW_TILE, 1, TOP_K * ROW_TILE), lambda ph, i: (i * ph, 0, 0)),
                   const((1, nb_pad)), const((1, LANES))],
        out_shape=[jax.ShapeDtypeStruct((8, t), F32),
                   jax.ShapeDtypeStruct((t // ROW_TILE, 1, TOP_K * ROW_TILE), I32),
                   jax.ShapeDtypeStruct((1, nb_pad), I32),
                   jax.ShapeDtypeStruct((1, LANES), I32)],
        scratch_shapes=[pltpu.VMEM((n_exp, 1), F32), pltpu.VMEM((n_exp, 1), F32)],
        compiler_params=_cparams(("arbitrary", "arbitrary"), VMEM_LIMIT),
        name="route_plan",
    )(logits_t, router_b.reshape(n_exp, 1).astype(F32), tri, low)
    return gates, pos3, be, nu, nb


def _row_dma_loop(n_rows, make_copies):
    def body(it, carry):
        for u in range(DMA_UNROLL):
            for k, cp in enumerate(make_copies(it * DMA_UNROLL + u)):
                cp.start(priority=k % 2)
        return carry
    lax.fori_loop(0, n_rows // DMA_UNROLL, body, 0)


def _to_tile_rows(dst_ref, mat):
    m = mat.shape[0]
    for s in range(mat.shape[1] // LANES):
        dst_ref[pl.ds(s, m, stride=8), :] = mat[:, s * LANES:(s + 1) * LANES]


def _tile_row_chunk(src_ref, s, m):
    return src_ref[pl.ds(s, m, stride=8), :]


def _dispatch_kernel(pos_ref, hp_ref, xs_in_ref, xs_ref, sbuf, sem, *, n_tiles):
    del xs_in_ref
    i = pl.program_id(0)
    slot = i % 2

    def wait_slot(s):
        for _ in range(TOP_K):
            pltpu.make_async_copy(sbuf.at[s], xs_ref.at[pl.ds(0, ROW_TILE * 8)], sem.at[s]).wait()

    @pl.when(i >= 2)
    def _():
        wait_slot(slot)

    _to_tile_rows(sbuf.at[slot], hp_ref[...])

    def copies(r):
        return [pltpu.make_async_copy(
            sbuf.at[slot, pl.ds(pl.multiple_of(r * 8, 8), 8)],
            xs_ref.at[pl.ds(pl.multiple_of(pos_ref[0, 0, k * ROW_TILE + r] * 8, 8), 8)], sem.at[slot])
            for k in range(TOP_K)]

    _row_dma_loop(ROW_TILE, copies)

    @pl.when(i == n_tiles - 1)
    def _():
        wait_slot(slot)
        if n_tiles >= 2:
            wait_slot(1 - slot)


def _dispatch(hp, pos3, n_slots):
    t, half = hp.shape
    assert half == 8 * LANES, "a packed row must be exactly one (8, 128) tile"
    n_tiles = t // ROW_TILE
    xs0 = jnp.zeros((n_slots * 8, LANES), I32)
    return pl.pallas_call(
        functools.partial(_dispatch_kernel, n_tiles=n_tiles),
        grid=(n_tiles,),
        in_specs=[pl.BlockSpec((1, 1, TOP_K * ROW_TILE), lambda i: (i, 0, 0), memory_space=pltpu.SMEM),
                  pl.BlockSpec((ROW_TILE, half), lambda i: (i, 0)),
                  pl.BlockSpec(memory_space=pl.ANY)],
        out_specs=pl.BlockSpec(memory_space=pl.ANY),
        out_shape=jax.ShapeDtypeStruct((n_slots * 8, LANES), I32),
        scratch_shapes=[pltpu.VMEM((2, ROW_TILE * 8, LANES), I32), pltpu.SemaphoreType.DMA((2,))],
        input_output_aliases={2: 0},
        compiler_params=_cparams(("arbitrary",), VMEM_LIMIT),
        name="moe_dispatch",
    )(pos3, hp, xs0)


def _moe_kernel(be_ref, nu_ref, x_ref, wg_ref, wu_ref, wd_ref, y_ref):
    b = pl.program_id(0)
    half = wg_ref.shape[1] // 2
    n_used = nu_ref[0, 0]

    @pl.when(b < n_used)
    def _():
        u = jnp.concatenate([_tile_row_chunk(x_ref, s, MOE_BLK) for s in range(half // LANES)], axis=1)
        x_hi = _unpack_hi(u).astype(BF16)
        x_lo = _unpack_lo(u).astype(BF16)

        def proj(w_ref):
            return (jnp.dot(x_hi, w_ref[0, :half, :], preferred_element_type=F32)
                    + jnp.dot(x_lo, w_ref[0, half:, :], preferred_element_type=F32))

        act = (_silu(proj(wg_ref)) * proj(wu_ref)).astype(BF16)
        _to_tile_rows(y_ref, _pack_pairs(jnp.dot(act, wd_ref[0], preferred_element_type=F32)))

    @pl.when(b >= n_used)
    def _():
        y_ref[...] = jnp.zeros(y_ref.shape, y_ref.dtype)


def _moe(xs, be, n_used, wg, wu, wd):
    nb = xs.shape[0] // (MOE_BLK * 8)
    _, d, d_exp = wg.shape
    w_idx = lambda b, be, nu: (be[0, b], 0, 0)
    row_blk = pl.BlockSpec((MOE_BLK * 8, LANES), lambda b, be, nu: (b, 0))
    grid_spec = pltpu.PrefetchScalarGridSpec(
        num_scalar_prefetch=2,
        grid=(nb,),
        in_specs=[row_blk,
                  pl.BlockSpec((1, d, d_exp), w_idx),
                  pl.BlockSpec((1, d, d_exp), w_idx),
                  pl.BlockSpec((1, d_exp, d), w_idx)],
        out_specs=row_blk)
    return pl.pallas_call(
        _moe_kernel,
        grid_spec=grid_spec,
        out_shape=jax.ShapeDtypeStruct(xs.shape, I32),
        compiler_params=_cparams(("arbitrary",), VMEM_LIMIT),
        name="moe_experts",
    )(be, n_used, xs, wg, wu, wd)


def _combine_kernel(pos_ref, posn_ref, x_ref, mod_ref, gt_ref, y_hbm, o_ref, ybuf, sem, *, d, n_tiles):
    i = pl.program_id(0)
    slot = i % 2
    half = d // 2

    def issue(p_ref, s):
        def copies(r):
            return [pltpu.make_async_copy(
                y_hbm.at[pl.ds(pl.multiple_of(p_ref[0, 0, k * ROW_TILE + r] * 8, 8), 8)],
                ybuf.at[s, k, pl.ds(pl.multiple_of(r * 8, 8), 8)], sem.at[s])
                for k in range(TOP_K)]
        _row_dma_loop(ROW_TILE, copies)

    @pl.when(i == 0)
    def _():
        issue(pos_ref, 0)

    @pl.when(i + 1 < n_tiles)
    def _():
        issue(posn_ref, 1 - slot)

    for k in range(TOP_K):
        pltpu.make_async_copy(y_hbm.at[pl.ds(0, ROW_TILE * 8)], ybuf.at[slot, k], sem.at[slot]).wait()

    w0, w1 = gt_ref[:, 0:1], gt_ref[:, 1:2]
    for s in range(half // LANES):
        u0 = _tile_row_chunk(ybuf.at[slot, 0], s, ROW_TILE)
        u1 = _tile_row_chunk(ybuf.at[slot, 1], s, ROW_TILE)
        for off, unpack in ((0, _unpack_hi), (half, _unpack_lo)):
            lo, hi = off + s * LANES, off + (s + 1) * LANES
            f = w0 * unpack(u0) + w1 * unpack(u1)
            o_ref[:, lo:hi] = x_ref[:, lo:hi] + mod_ref[0, :, 5 * d + lo:5 * d + hi] * f


def _combine(x_mid, mod3, layer, gates, pos3, y, *, n_x_tiles, tiles_per_seq):
    t, d = x_mid.shape
    n_batch = n_x_tiles // tiles_per_seq
    n_tiles = t // ROW_TILE
    half = d // 2

    def mod_idx(i):
        return (layer * 8 + jnp.where(i < n_x_tiles, i // tiles_per_seq, n_batch), 0, 0)

    smem_blk = lambda f: pl.BlockSpec((1, 1, TOP_K * ROW_TILE), f, memory_space=pltpu.SMEM)
    return pl.pallas_call(
        functools.partial(_combine_kernel, d=d, n_tiles=n_tiles),
        grid=(n_tiles,),
        in_specs=[smem_blk(lambda i: (i, 0, 0)),
                  smem_blk(lambda i: (jnp.minimum(i + 1, n_tiles - 1), 0, 0)),
                  pl.BlockSpec((ROW_TILE, d), lambda i: (i, 0)),
                  pl.BlockSpec((1, 1, mod3.shape[2]), mod_idx),
                  pl.BlockSpec((ROW_TILE, TOP_K), lambda i: (i, 0)),
                  pl.BlockSpec(memory_space=pl.ANY)],
        out_specs=pl.BlockSpec((ROW_TILE, d), lambda i: (i, 0)),
        out_shape=jax.ShapeDtypeStruct((t, d), F32),
        scratch_shapes=[pltpu.VMEM((2, TOP_K, ROW_TILE * 8, LANES), I32), pltpu.SemaphoreType.DMA((2,))],
        compiler_params=_cparams(("arbitrary",), VMEM_LIMIT),
        name="moe_combine",
    )(pos3, pos3, x_mid, mod3, gates, y)


def _rope_tables(seq, tm):
    nf = HEAD_DIM // 4
    inv = np.power(np.float32(ROPE_BASE), -np.arange(nf, dtype=np.float32) / np.float32(nf)).astype(np.float32)
    tt = np.arange(seq)
    ar = (tt // GRID_W).astype(np.float32)[:, None] * inv
    ac = (tt % GRID_W).astype(np.float32)[:, None] * inv
    ang = np.concatenate([ar, ar, ac, ac], axis=-1)
    cos = np.concatenate([np.cos(ang), np.ones((tm, HEAD_DIM), np.float32)], axis=0)
    sin = np.concatenate([np.sin(ang), np.zeros((tm, HEAD_DIM), np.float32)], axis=0)
    return jnp.asarray(cos, F32), jnp.asarray(sin, F32)


def kernel(x, c, ctx, c_ctx, ada_w, ada_b, norm_mix_g, norm_ffn_g, ab_w_in, ab_w_out, ab_q_norm, ab_k_norm, ab_sink, conv_w, conv_b, conv_ln_g, conv_ln_b, na_w_in, na_w_out, na_q_norm, na_k_norm, na_rpb, router_w, router_b, moe_w_gate, moe_w_up, moe_w_down):
    batch, seq, d = x.shape
    ctx_len = ctx.shape[1]
    depth = ada_w.shape[0]
    n_exp = router_w.shape[1]
    b_ch = conv_w.shape[-1]
    a_qw = ab_w_out.shape[1] - b_ch
    a_kvw = (ab_w_in.shape[-1] - a_qw - 2 * b_ch) // 2
    n_kv = 2
    na_w = na_w_out.shape[1]
    assert batch + 1 <= 8 and a_qw == b_ch and n_exp % 8 == 0

    tx, tc = batch * seq, batch * ctx_len
    t = tx + tc
    tm_in = next(m for m in (1024, 512, 256) if seq % m == 0 and tc % m == 0)
    tn = 512 if d >= 2048 else 256

    c8 = jnp.zeros((8, d), F32).at[:batch].set(c).at[batch].set(c_ctx)
    mod3 = _ada_all(c8, ada_w, ada_b).reshape(depth * 8, 1, 6 * d)

    cos, sin = _rope_tables(seq, tm_in)
    rw = jnp.zeros((d, LANES), F32).at[:, :n_exp].set(router_w.astype(F32))
    rw_hi = rw.astype(BF16)
    rw_cat = jnp.concatenate([rw_hi, (rw - rw_hi.astype(F32)).astype(BF16)], axis=1)
    rows = seq // GRID_W

    k0, v0, u0 = a_qw, a_qw + a_kvw, a_qw + 2 * a_kvw
    k_off, v_off = a_qw + 2 * b_ch, a_qw + 2 * b_ch + a_kvw
    ab_w_in_b = jnp.concatenate([ab_w_in[..., :k0], ab_w_in[..., u0:], ab_w_in[..., k0:u0]], axis=-1).astype(BF16)
    ab_w_out_b = ab_w_out.astype(BF16)
    na_w_in_b = na_w_in.astype(BF16)
    na_w_out_b = na_w_out.astype(BF16)
    d_exp = moe_w_gate.shape[-1]
    wg_all = moe_w_gate.astype(BF16).reshape(depth * n_exp, d, d_exp)
    wu_all = moe_w_up.astype(BF16).reshape(depth * n_exp, d, d_exp)
    wd_all = moe_w_down.astype(BF16).reshape(depth * n_exp, d_exp, d)

    def ab_kind(col):
        return "q" if col < a_qw else ("k" if k_off <= col < v_off else "p")

    def na_kind(col):
        return "q" if col < na_w else ("k" if col < 2 * na_w else "p")

    tok = jnp.concatenate([x.reshape(tx, d), ctx.reshape(tc, d)], axis=0)

    for i in range(depth):
        with_ctx = i < depth - 1
        j = i // 2
        t_act = t if with_ctx else tx
        tiles = dict(n_x_tiles=tx // tm_in, tiles_per_seq=seq // tm_in)
        if i % 2 == 0:
            px = _inproj(tok, mod3, i, norm_mix_g[i], ab_w_in_b, j, ab_q_norm[j], ab_k_norm[j], ab_kind, cos, sin,
                         tm=tm_in, tn=tn, **tiles)
            att = _win_attn(px, ab_sink[j], batch=batch, seq=seq, ctx_len=ctx_len, a_qw=a_qw, n_kv=n_kv,
                            k_off=k_off, v_off=v_off, with_ctx=with_ctx)
            cv = _conv(px, conv_w[j], conv_b[j], conv_ln_g[j], conv_ln_b[j], a_off=a_qw, g_off=a_qw + b_ch,
                       batch=batch, seq=seq, ctx_len=ctx_len, with_ctx=with_ctx, tmc=256)
            acts, w_out = [att, cv], ab_w_out_b
        else:
            px = _inproj(tok, mod3, i, norm_mix_g[i], na_w_in_b, j, na_q_norm[j], na_k_norm[j], na_kind, None, None,
                         tm=tm_in, tn=tn, **tiles)
            bias = _na_bias_table(na_rpb[j], rows)
            att = _na_attn(px, bias, batch=batch, seq=seq, ctx_len=ctx_len, n_heads=na_w // HEAD_DIM,
                           with_ctx=with_ctx)
            acts, w_out = [att], na_w_out_b

        otiles = dict(n_x_tiles=tx // ROW_TILE, tiles_per_seq=seq // ROW_TILE)
        x_mid, hp, logits_t = _outproj(tok, mod3, i, norm_ffn_g[i], rw_cat, rw_hi, acts, w_out, j, t_act=t_act,
                                       n_exp=n_exp, tm=ROW_TILE, **otiles)
        g_out, pos3, be, n_used, nb = _route_plan(logits_t, router_b, i * n_exp)
        xs = _dispatch(hp, pos3, nb * MOE_BLK)
        y = _moe(xs, be, n_used, wg_all, wu_all, wd_all)
        tok = _combine(x_mid, mod3, i, g_out[:TOP_K].T, pos3, y, **otiles)

    return tok[:tx].reshape(batch, seq, d)
```

```python
import functools

import numpy as np
import jax
import jax.numpy as jnp
from jax import lax
from jax.experimental import pallas as pl
from jax.experimental.pallas import tpu as pltpu

F32 = jnp.float32
BF16 = jnp.bfloat16
I32 = jnp.int32

HEAD_DIM = 128
LANES = 128
GRID_W = 64
WINDOW = 128
N_GROUPS = 4
TOP_K = 2
ROPE_BASE = 10000.0
EPS = 1e-6
NEG_INF = -1e30
MOE_BLK = 256
ROW_TILE = 256
NA_RB = 4
NA_HG = 8
HALO = 16
DMA_UNROLL = 64
VMEM_LIMIT = 56 * 1024 * 1024
HI_MASK = -65536
LOG2E = 1.4426950408889634
Q_SCALE = HEAD_DIM ** -0.5 * LOG2E


def _cparams(sem, vmem=None):
    return pltpu.CompilerParams(dimension_semantics=sem, vmem_limit_bytes=vmem)


def _silu(v):
    return v * jax.nn.sigmoid(v)


def _rms(v, g):
    ms = jnp.mean(v * v, axis=-1, keepdims=True)
    return v * lax.rsqrt(ms + EPS) * g


def _pack_pairs(v):
    half = v.shape[1] // 2
    bits = pltpu.bitcast(v.astype(BF16).astype(F32), I32)
    return (bits[:, :half] & HI_MASK) | lax.shift_right_logical(bits[:, half:], 16)


def _unpack_hi(u):
    return pltpu.bitcast(u & HI_MASK, F32)


def _unpack_lo(u):
    return pltpu.bitcast(lax.shift_left(u, 16), F32)


def _ada_kernel(c_ref, w_ref, b_ref, o_ref):
    sc = _silu(c_ref[...])
    o_ref[0] = jnp.dot(sc.astype(BF16), w_ref[0].astype(BF16), preferred_element_type=F32) + b_ref[0]


def _ada_all(c8, ada_w, ada_b):
    depth, d, n = ada_w.shape
    tn = min(n, 1024)
    return pl.pallas_call(
        _ada_kernel,
        grid=(depth, n // tn),
        in_specs=[pl.BlockSpec((8, d), lambda l, j: (0, 0)),
                  pl.BlockSpec((1, d, tn), lambda l, j: (l, 0, j)),
                  pl.BlockSpec((1, 1, tn), lambda l, j: (l, 0, j))],
        out_specs=pl.BlockSpec((1, 8, tn), lambda l, j: (l, 0, j)),
        out_shape=jax.ShapeDtypeStruct((depth, 8, n), F32),
        compiler_params=_cparams(("arbitrary", "arbitrary"), VMEM_LIMIT),
        name="ada_mod",
    )(c8, ada_w, ada_b.reshape(depth, 1, n))


def _rope(y, cos, sin):
    lane = lax.broadcasted_iota(I32, y.shape, 1)
    first = (lane & 32) == 0
    fwd = pltpu.roll(y, 32, 1)
    bwd = pltpu.roll(y, 96, 1)
    return y * cos + jnp.where(first, -bwd, fwd) * sin


def _inproj_kernel(*refs, d, tn, groups, rope):
    if rope:
        x_ref, mod_ref, g_ref, w_ref, qg_ref, kg_ref, cos_ref, sin_ref, o_ref, h_scr = refs
    else:
        x_ref, mod_ref, g_ref, w_ref, qg_ref, kg_ref, o_ref, h_scr = refs
        cos_ref = sin_ref = None
    j = pl.program_id(1)

    @pl.when(j == 0)
    def _():
        y = _rms(x_ref[...], g_ref[...])
        shift = mod_ref[0, :, 0:d]
        scale = mod_ref[0, :, d:2 * d]
        h_scr[...] = (y * (1.0 + scale) + shift).astype(BF16)

    acc = jnp.dot(h_scr[...], w_ref[...], preferred_element_type=F32)

    for kinds, lo, hi in groups:
        @pl.when((j >= lo) & (j <= hi))
        def _(kinds=kinds):
            if all(k == "p" for k in kinds):
                o_ref[...] = acc.astype(o_ref.dtype)
                return
            for s, kind in enumerate(kinds):
                piece = acc[:, s * LANES:(s + 1) * LANES]
                if kind != "p":
                    piece = _rms(piece, (qg_ref if kind == "q" else kg_ref)[...])
                    if rope:
                        piece = _rope(piece, cos_ref[...], sin_ref[...])
                    if kind == "q":
                        piece = piece * Q_SCALE
                o_ref[:, s * LANES:(s + 1) * LANES] = piece.astype(o_ref.dtype)


def _inproj(tok, mod3, layer, norm_g, w, w_layer, qg, kg, kind_of_col, cos, sin, *, n_x_tiles, tiles_per_seq, tm, tn):
    t, d = tok.shape
    n = w.shape[2]
    nj = n // tn
    per_j = [tuple(kind_of_col(j * tn + s * LANES) for s in range(tn // LANES)) for j in range(nj)]
    groups = []
    for j, kinds in enumerate(per_j):
        if groups and groups[-1][0] == kinds and groups[-1][2] == j - 1:
            groups[-1] = (kinds, groups[-1][1], j)
        else:
            groups.append((kinds, j, j))
    rope = cos is not None
    n_batch = n_x_tiles // tiles_per_seq

    def mod_idx(i, j):
        return (layer * 8 + jnp.where(i < n_x_tiles, i // tiles_per_seq, n_batch), 0, 0)

    def pos_idx(i, j):
        return (jnp.where(i < n_x_tiles, i % tiles_per_seq, tiles_per_seq), 0)

    in_specs = [pl.BlockSpec((tm, d), lambda i, j: (i, 0)),
                pl.BlockSpec((1, 1, mod3.shape[2]), mod_idx),
                pl.BlockSpec((1, d), lambda i, j: (0, 0)),
                pl.BlockSpec((None, d, tn), lambda i, j: (w_layer, 0, j)),
                pl.BlockSpec((1, HEAD_DIM), lambda i, j: (0, 0)),
                pl.BlockSpec((1, HEAD_DIM), lambda i, j: (0, 0))]
    args = [tok, mod3, norm_g.reshape(1, d), w, qg.reshape(1, HEAD_DIM), kg.reshape(1, HEAD_DIM)]
    if rope:
        in_specs += [pl.BlockSpec((tm, HEAD_DIM), pos_idx), pl.BlockSpec((tm, HEAD_DIM), pos_idx)]
        args += [cos, sin]
    return pl.pallas_call(
        functools.partial(_inproj_kernel, d=d, tn=tn, groups=tuple(groups), rope=rope),
        grid=(t // tm, nj),
        in_specs=in_specs,
        out_specs=pl.BlockSpec((tm, tn), lambda i, j: (i, j)),
        out_shape=jax.ShapeDtypeStruct((t, n), BF16),
        scratch_shapes=[pltpu.VMEM((tm, d), BF16)],
        compiler_params=_cparams(("arbitrary", "arbitrary"), VMEM_LIMIT),
        name="in_proj",
    )(*args)


def _softmax_pv(s, v, extra_logit=None):
    m = jnp.max(s, axis=-1, keepdims=True)
    if extra_logit is not None:
        m = jnp.maximum(m, extra_logit)
    p = jnp.exp2(s - m)
    den = jnp.sum(p, axis=-1, keepdims=True)
    if extra_logit is not None:
        den = den + jnp.exp2(extra_logit - m)
    o = jnp.dot(p.astype(BF16), v, preferred_element_type=F32)
    return o / den


def _win_attn_kernel(sink_ref, q_ref, kp_ref, kc_ref, kn_ref, vp_ref, vc_ref, vn_ref, kx_ref, vx_ref, o_ref,
                     *, n_grp, nb):
    h = pl.program_id(1)
    n = pl.program_id(2)
    w = WINDOW
    q = q_ref[...]
    qs = jnp.concatenate([q[:, g * HEAD_DIM:(g + 1) * HEAD_DIM] for g in range(n_grp)], axis=0)
    k = jnp.concatenate([kp_ref[...], kc_ref[...], kn_ref[...], kx_ref[...]], axis=0)
    v = jnp.concatenate([vp_ref[...], vc_ref[...], vn_ref[...], vx_ref[...]], axis=0)
    s = lax.dot_general(qs, k, (((1,), (1,)), ((), ())), preferred_element_type=F32)
    rows = lax.broadcasted_iota(I32, s.shape, 0) & (w - 1)
    cols = lax.broadcasted_iota(I32, s.shape, 1)
    is_x = n < nb
    lo = jnp.where(is_x, jnp.where(n > 0, 0, w), 0)
    hi = jnp.where(is_x, jnp.where(n < nb - 1, 3 * w, 2 * w), 0)
    local_ok = (jnp.abs(cols - w - rows) <= WINDOW) & (cols >= lo) & (cols < hi)
    s = jnp.where(local_ok | (cols >= 3 * w), s, NEG_INF)
    for g in range(n_grp):
        o = _softmax_pv(s[g * w:(g + 1) * w], v, sink_ref[h, g] * LOG2E)
        o_ref[:, g * HEAD_DIM:(g + 1) * HEAD_DIM] = o.astype(o_ref.dtype)


def _win_attn(px, sink, *, batch, seq, ctx_len, a_qw, n_kv, k_off, v_off, with_ctx):
    t = px.shape[0]
    w = WINDOW
    n_grp = a_qw // HEAD_DIM // n_kv
    nb = seq // w
    nq = nb + (ctx_len // w if with_ctx else 0)
    qw = n_grp * HEAD_DIM
    ctx_blk0 = batch * seq // ctx_len

    def q_idx(b, h, n):
        return (jnp.where(n < nb, b * nb + n, batch * nb + b * (ctx_len // w) + (n - nb)), h)

    def kv_idx(off, delta):
        def f(b, h, n):
            return (b * nb + jnp.clip(n + delta, 0, nb - 1), off // HEAD_DIM + h)
        return f

    def ctx_idx(off):
        return lambda b, h, n: (ctx_blk0 + b, off // HEAD_DIM + h)

    blk = lambda f: pl.BlockSpec((w, HEAD_DIM), f)
    in_specs = [pl.BlockSpec(memory_space=pltpu.SMEM),
                pl.BlockSpec((w, qw), q_idx),
                blk(kv_idx(k_off, -1)), blk(kv_idx(k_off, 0)), blk(kv_idx(k_off, 1)),
                blk(kv_idx(v_off, -1)), blk(kv_idx(v_off, 0)), blk(kv_idx(v_off, 1)),
                pl.BlockSpec((ctx_len, HEAD_DIM), ctx_idx(k_off)),
                pl.BlockSpec((ctx_len, HEAD_DIM), ctx_idx(v_off))]
    return pl.pallas_call(
        functools.partial(_win_attn_kernel, n_grp=n_grp, nb=nb),
        grid=(batch, n_kv, nq),
        in_specs=in_specs,
        out_specs=pl.BlockSpec((w, qw), q_idx),
        out_shape=jax.ShapeDtypeStruct((t if with_ctx else batch * seq, a_qw), BF16),
        compiler_params=_cparams(("arbitrary",) * 3, VMEM_LIMIT),
        name="win_attn",
    )(sink.reshape(n_kv, n_grp).astype(F32), *([px] * 9))


def _na_attn_kernel(q_ref, kp_ref, kc_ref, kn_ref, vp_ref, vc_ref, vn_ref, kx_ref, vx_ref, bias_ref, o_ref,
                    *, n_hg):
    nloc = 3 * NA_RB * GRID_W
    for hh in range(n_hg):
        cl = slice(hh * HEAD_DIM, (hh + 1) * HEAD_DIM)
        k = jnp.concatenate([kp_ref[:, cl], kc_ref[:, cl], kn_ref[:, cl], kx_ref[:, cl]], axis=0)
        v = jnp.concatenate([vp_ref[:, cl], vc_ref[:, cl], vn_ref[:, cl], vx_ref[:, cl]], axis=0)
        s = lax.dot_general(q_ref[:, cl], k, (((1,), (1,)), ((), ())), preferred_element_type=F32)
        s = jnp.concatenate([s[:, :nloc] + bias_ref[hh, 0], s[:, nloc:]], axis=1)
        o_ref[:, cl] = _softmax_pv(s, v).astype(o_ref.dtype)


def _na_bias_table(rpb, rows):
    n_heads, n_dr, n_dc = rpb.shape
    kh, kw = (n_dr + 1) // 2, (n_dc + 1) // 2
    n_rb = rows // NA_RB
    cidx = np.arange(GRID_W)
    cs = np.clip(cidx - kw // 2, 0, GRID_W - kw)
    col_ok = (cidx[None, :] >= cs[:, None]) & (cidx[None, :] < cs[:, None] + kw)
    dc_idx = np.clip(cidx[None, :] - cidx[:, None], -(kw - 1), kw - 1) + kw - 1
    a = jnp.where(col_ok[None, None], rpb.astype(F32)[:, :, dc_idx] * LOG2E, NEG_INF)
    masked = jnp.full((n_heads, GRID_W, GRID_W), NEG_INF, F32)
    classes = []
    for rb in (0, min(1, n_rb - 1), n_rb - 1):
        qrows = []
        for j in range(NA_RB):
            r = rb * NA_RB + j
            rs = int(np.clip(r - kh // 2, 0, rows - kh))
            blocks = []
            for tblk in range(3):
                for krl in range(NA_RB):
                    kr = (rb - 1 + tblk) * NA_RB + krl
                    ok = (rs <= kr < rs + kh) and (0 <= kr < rows)
                    blocks.append(a[:, kr - r + kh - 1] if ok else masked)
            qrows.append(jnp.concatenate(blocks, axis=-1))
        classes.append(jnp.concatenate(qrows, axis=1))
    classes.append(jnp.full_like(classes[0], NEG_INF))
    return jnp.stack(classes, axis=1)


def _na_attn(px, bias, *, batch, seq, ctx_len, n_heads, with_ctx):
    t = px.shape[0]
    qb = NA_RB * GRID_W
    assert ctx_len == qb, "context queries are processed as one extra query block"
    n_rb = seq // qb
    nq = n_rb + (1 if with_ctx else 0)
    na_w = n_heads * HEAD_DIM
    n_hg = min(NA_HG, n_heads)
    gw = n_hg * HEAD_DIM
    assert n_heads % n_hg == 0
    ctx_blk0 = batch * seq // ctx_len

    def q_idx(b, h, r):
        return (jnp.where(r < n_rb, b * n_rb + r, batch * n_rb + b), h)

    def kv_idx(off, delta):
        return lambda b, h, r: (b * n_rb + jnp.clip(r + delta, 0, n_rb - 1), off // gw + h)

    def ctx_idx(off):
        return lambda b, h, r: (ctx_blk0 + b, off // gw + h)

    def bias_idx(b, h, r):
        return (h, jnp.where(r == 0, 0, jnp.where(r < n_rb - 1, 1, jnp.where(r == n_rb - 1, 2, 3))), 0, 0)

    blk = lambda f: pl.BlockSpec((qb, gw), f)
    in_specs = [blk(q_idx),
                blk(kv_idx(na_w, -1)), blk(kv_idx(na_w, 0)), blk(kv_idx(na_w, 1)),
                blk(kv_idx(2 * na_w, -1)), blk(kv_idx(2 * na_w, 0)), blk(kv_idx(2 * na_w, 1)),
                pl.BlockSpec((ctx_len, gw), ctx_idx(na_w)),
                pl.BlockSpec((ctx_len, gw), ctx_idx(2 * na_w)),
                pl.BlockSpec((n_hg, 1, qb, 3 * qb), bias_idx)]
    return pl.pallas_call(
        functools.partial(_na_attn_kernel, n_hg=n_hg),
        grid=(batch, n_heads // n_hg, nq),
        in_specs=in_specs,
        out_specs=blk(q_idx),
        out_shape=jax.ShapeDtypeStruct((t if with_ctx else batch * seq, na_w), BF16),
        compiler_params=_cparams(("arbitrary",) * 3, VMEM_LIMIT),
        name="na_attn",
    )(*([px] * 9), bias)


def _conv_kernel(a_ref, g_ref, ap_ref, gp_ref, an_ref, gn_ref, w_ref, b_ref, lg_ref, lb_ref, o_ref, hbuf, hs, cbuf,
                 *, tiles_per_seq, n_x_tiles, n_taps, tmc, sub):
    i = pl.program_id(0)
    p = i % tiles_per_seq
    is_x = i < n_x_tiles
    has_prev = is_x & (p > 0)
    has_next = is_x & (p < tiles_per_seq - 1)

    def glu(a, g):
        return a.astype(F32) * jax.nn.sigmoid(g.astype(F32))

    hbuf[0:HALO, :] = jnp.where(has_prev, glu(ap_ref[...], gp_ref[...]), 0.0)
    hbuf[HALO:HALO + tmc, :] = glu(a_ref[...], g_ref[...])
    hbuf[HALO + tmc:2 * HALO + tmc, :] = jnp.where(has_next, glu(an_ref[...], gn_ref[...]), 0.0)

    n_buf = tmc + 2 * HALO
    for sh in range(1, 8):
        hs[sh - 1, 0:n_buf - 8, :] = hbuf[sh:sh + n_buf - 8, :]

    ch = a_ref.shape[1]
    first = HALO - n_taps // 2
    for c in range(ch // LANES):
        cl = slice(c * LANES, (c + 1) * LANES)
        for tb in range(tmc // sub):
            acc = jnp.zeros((sub, LANES), F32)
            for k in range(n_taps):
                sh = (first + k) % 8
                r0 = tb * sub + first + k - sh
                slab = hbuf[r0:r0 + sub, cl] if sh == 0 else hs[sh - 1, r0:r0 + sub, cl]
                acc = acc + slab * w_ref[k:k + 1, cl]
            cbuf[tb * sub:(tb + 1) * sub, cl] = acc + b_ref[:, cl]

    y = cbuf[...]
    mu = jnp.mean(y, axis=-1, keepdims=True)
    yc = y - mu
    var = jnp.mean(yc * yc, axis=-1, keepdims=True)
    yn = yc * lax.rsqrt(var + EPS) * lg_ref[...] + lb_ref[...]
    o_ref[...] = _silu(yn).astype(o_ref.dtype)


def _conv(px, conv_w, conv_b, ln_g, ln_b, *, a_off, g_off, batch, seq, ctx_len, with_ctx, tmc):
    t = px.shape[0]
    n_taps, ch = conv_w.shape
    assert n_taps // 2 <= HALO and ctx_len == tmc and seq % tmc == 0
    tiles_per_seq = seq // tmc
    n_x_tiles = batch * tiles_per_seq
    n_tiles = n_x_tiles + (batch if with_ctx else 0)
    hpt = tmc // HALO
    n_hblk = t // HALO
    w_pad = jnp.zeros((32, ch), F32).at[:n_taps].set(conv_w.astype(F32))

    main = lambda off: pl.BlockSpec((tmc, ch), lambda i: (i, off // ch))
    prev = lambda off: pl.BlockSpec((HALO, ch), lambda i: (jnp.maximum(i * hpt - 1, 0), off // ch))
    nxt = lambda off: pl.BlockSpec((HALO, ch), lambda i: (jnp.minimum((i + 1) * hpt, n_hblk - 1), off // ch))
    vec = lambda: pl.BlockSpec((1, ch), lambda i: (0, 0))
    return pl.pallas_call(
        functools.partial(_conv_kernel, tiles_per_seq=tiles_per_seq, n_x_tiles=n_x_tiles, n_taps=n_taps,
                          tmc=tmc, sub=64),
        grid=(n_tiles,),
        in_specs=[main(a_off), main(g_off), prev(a_off), prev(g_off), nxt(a_off), nxt(g_off),
                  pl.BlockSpec((32, ch), lambda i: (0, 0)), vec(), vec(), vec()],
        out_specs=pl.BlockSpec((tmc, ch), lambda i: (i, 0)),
        out_shape=jax.ShapeDtypeStruct((t if with_ctx else batch * seq, ch), BF16),
        scratch_shapes=[pltpu.VMEM((tmc + 2 * HALO, ch), F32), pltpu.VMEM((7, tmc + 2 * HALO, ch), F32),
                        pltpu.VMEM((tmc, ch), F32)],
        compiler_params=_cparams(("arbitrary",), VMEM_LIMIT),
        name="conformer_conv",
    )(px, px, px, px, px, px, w_pad, conv_b.reshape(1, ch).astype(F32), ln_g.reshape(1, ch).astype(F32),
      ln_b.reshape(1, ch).astype(F32))


def _outproj_kernel(*refs, d, n_in, n_exp):
    x_ref, mod_ref, g_ref, rwc_ref, rwh_ref = refs[:5]
    a_refs = refs[5:5 + n_in]
    w_refs = refs[5 + n_in:5 + 2 * n_in]
    xo_ref, hp_ref, lt_ref = refs[5 + 2 * n_in:]
    o = jnp.dot(a_refs[0][...], w_refs[0][...], preferred_element_type=F32)
    for a_ref, w_ref in zip(a_refs[1:], w_refs[1:]):
        o = o + jnp.dot(a_ref[...], w_ref[...], preferred_element_type=F32)
    x_new = x_ref[...] + mod_ref[0, :, 2 * d:3 * d] * o
    xo_ref[...] = x_new
    hf = _rms(x_new, g_ref[...]) * (1.0 + mod_ref[0, :, 4 * d:5 * d]) + mod_ref[0, :, 3 * d:4 * d]
    hp_ref[...] = _pack_pairs(hf)
    h_hi = hf.astype(BF16)
    h_lo = (hf - h_hi.astype(F32)).astype(BF16)
    both = jnp.dot(h_hi, rwc_ref[...], preferred_element_type=F32)
    lg = both[:, :LANES] + both[:, LANES:] + jnp.dot(h_lo, rwh_ref[...], preferred_element_type=F32)
    lt_ref[...] = lg.T[:n_exp]


def _outproj(tok, mod3, layer, norm_g, rw_cat, rw_hi, acts, w, w_layer, *, t_act, n_exp, n_x_tiles, tiles_per_seq, tm):
    d = tok.shape[1]
    n_batch = n_x_tiles // tiles_per_seq
    n_in = len(acts)

    def mod_idx(i):
        return (layer * 8 + jnp.where(i < n_x_tiles, i // tiles_per_seq, n_batch), 0, 0)

    in_specs = [pl.BlockSpec((tm, d), lambda i: (i, 0)),
                pl.BlockSpec((1, 1, mod3.shape[2]), mod_idx),
                pl.BlockSpec((1, d), lambda i: (0, 0)),
                pl.BlockSpec(rw_cat.shape, lambda i: (0, 0)),
                pl.BlockSpec(rw_hi.shape, lambda i: (0, 0))]
    in_specs += [pl.BlockSpec((tm, a.shape[1]), lambda i: (i, 0)) for a in acts]
    kw = acts[0].shape[1]
    assert all(a.shape[1] == kw for a in acts) and w.shape[1] == kw * n_in
    in_specs += [pl.BlockSpec((None, kw, d), lambda i, r=r: (w_layer, r, 0)) for r in range(n_in)]
    return pl.pallas_call(
        functools.partial(_outproj_kernel, d=d, n_in=n_in, n_exp=n_exp),
        grid=(t_act // tm,),
        in_specs=in_specs,
        out_specs=[pl.BlockSpec((tm, d), lambda i: (i, 0)),
                   pl.BlockSpec((tm, d // 2), lambda i: (i, 0)),
                   pl.BlockSpec((n_exp, tm), lambda i: (0, i))],
        out_shape=[jax.ShapeDtypeStruct((t_act, d), F32),
                   jax.ShapeDtypeStruct((t_act, d // 2), I32),
                   jax.ShapeDtypeStruct((n_exp, t_act), F32)],
        compiler_params=_cparams(("arbitrary",), VMEM_LIMIT),
        name="out_proj",
    )(tok, mod3, norm_g.reshape(1, d), rw_cat, rw_hi, *acts, *([w] * n_in))


def _route_tile(logits, bias, n_exp):
    epg = n_exp // N_GROUPS
    aff = jax.nn.sigmoid(logits)
    sel = aff + bias
    row = lambda a, i: a[i:i + 1, :]

    scores = []
    for g in range(N_GROUPS):
        best = None
        for i in range(epg):
            for j in range(i + 1, epg):
                pair = row(sel, g * epg + i) + row(sel, g * epg + j)
                best = pair if best is None else jnp.maximum(best, pair)
        scores.append(best)
    grp = jnp.zeros(scores[0].shape, I32)
    top = scores[0]
    for g in range(1, N_GROUPS):
        better = scores[g] > top
        grp = jnp.where(better, g, grp)
        top = jnp.where(better, scores[g], top)

    def pick(a, i):
        out = row(a, i)
        for g in range(1, N_GROUPS):
            out = jnp.where(grp == g, row(a, g * epg + i), out)
        return out

    v = [pick(sel, i) for i in range(epg)]
    a = [pick(aff, i) for i in range(epg)]
    ranks = []
    for i in range(epg):
        r = jnp.zeros(grp.shape, I32)
        for j in range(epg):
            if j != i:
                ahead = (v[j] > v[i]) | ((v[j] == v[i]) & (j < i)) if j < i else (v[j] > v[i])
                r = r + ahead.astype(I32)
        ranks.append(r)
    zero_i, zero_f = jnp.zeros(grp.shape, I32), jnp.zeros(grp.shape, F32)
    experts, affs = [], []
    for k in range(TOP_K):
        e_k, a_k = zero_i, zero_f
        for i in range(epg):
            hit = ranks[i] == k
            e_k = jnp.where(hit, i, e_k)
            a_k = jnp.where(hit, a[i], a_k)
        experts.append(grp * epg + e_k)
        affs.append(a_k)
    den = affs[0] + affs[1]
    return experts, [a_k / den for a_k in affs]


def _route_plan_kernel(l_ref, b_ref, tri_ref, low_ref, g_ref, pos_ref, be_ref, nu_ref, cnt_scr, carry_scr,
                       *, n_exp, tt, be_offset):
    ph = pl.program_id(0)
    i = pl.program_id(1)
    experts, gates = _route_tile(l_ref[...], b_ref[...], n_exp)
    eid = lax.broadcasted_iota(I32, (n_exp, tt), 0)
    onehot = [(eid == e_k).astype(F32) for e_k in experts]
    both = onehot[0] + onehot[1]
    tile_cnt = jnp.sum(both, axis=1, keepdims=True)

    @pl.when(ph == 0)
    def _():
        @pl.when(i == 0)
        def _():
            cnt_scr[...] = jnp.zeros(cnt_scr.shape, F32)
        cnt_scr[...] += tile_cnt

    @pl.when(ph == 1)
    def _():
        @pl.when(i == 0)
        def _():
            carry_scr[...] = jnp.zeros(carry_scr.shape, F32)
        blocks = jnp.floor((cnt_scr[...] + (MOE_BLK - 1)) * (1.0 / MOE_BLK))
        blocks_b = jnp.broadcast_to(blocks, (n_exp, LANES)).astype(BF16)
        first_blk = jnp.dot(low_ref[...], blocks_b, preferred_element_type=F32)[:, 0:1]
        prefix = jnp.dot(both.astype(BF16), tri_ref[...], preferred_element_type=F32)
        slot = first_blk * MOE_BLK + carry_scr[...] + prefix
        carry_scr[...] += tile_cnt
        pos = [jnp.sum(oh * slot, axis=0, keepdims=True).astype(I32) for oh in onehot]
        for q in range(tt // ROW_TILE):
            for k in range(TOP_K):
                pos_ref[q, :, k * ROW_TILE:(k + 1) * ROW_TILE] = pos[k][:, q * ROW_TILE:(q + 1) * ROW_TILE]
        g_ref[...] = jnp.concatenate(gates + [jnp.zeros_like(gates[0])] * (8 - TOP_K), axis=0)
        last_blk = first_blk + blocks
        bidx = lax.broadcasted_iota(I32, (n_exp, be_ref.shape[1]), 1).astype(F32)
        owner = jnp.sum((last_blk <= bidx).astype(F32), axis=0, keepdims=True)
        be_ref[...] = jnp.minimum(owner, n_exp - 1.0).astype(I32) + be_offset
        nu_ref[...] = jnp.broadcast_to(jnp.sum(blocks, axis=0, keepdims=True), nu_ref.shape).astype(I32)


def _route_plan(logits_t, router_b, be_offset):
    n_exp, t = logits_t.shape
    tt = next(m for m in (1024, 512, 256) if t % m == 0)
    n_tiles = t // tt
    nb = -(-(t * TOP_K + n_exp * (MOE_BLK - 1)) // MOE_BLK)
    nb_pad = -(-nb // LANES) * LANES
    tok_i = np.arange(tt)
    tri = jnp.asarray(tok_i[:, None] < tok_i[None, :], BF16)
    exp_i = np.arange(n_exp)
    low = jnp.asarray(exp_i[None, :] < exp_i[:, None], BF16)
    const = lambda shape: pl.BlockSpec(shape, lambda ph, i: (0,) * len(shape))
    gates, pos3, be, nu = pl.pallas_call(
        functools.partial(_route_plan_kernel, n_exp=n_exp, tt=tt, be_offset=be_offset),
        grid=(2, n_tiles),
        in_specs=[pl.BlockSpec((n_exp, tt), lambda ph, i: (0, i)),
                  const((n_exp, 1)), const((tt, tt)), const((n_exp, n_exp))],
        out_specs=[pl.BlockSpec((8, tt), lambda ph, i: (0, i * ph)),
                   pl.BlockSpec((tt // ROW_TILE, 1, TOP_K * ROW_TILE), lambda ph, i: (i * ph, 0, 0)),
                   const((1, nb_pad)), const((1, LANES))],
        out_shape=[jax.ShapeDtypeStruct((8, t), F32),
                   jax.ShapeDtypeStruct((t // ROW_TILE, 1, TOP_K * ROW_TILE), I32),
                   jax.ShapeDtypeStruct((1, nb_pad), I32),
                   jax.ShapeDtypeStruct((1, LANES), I32)],
        scratch_shapes=[pltpu.VMEM((n_exp, 1), F32), pltpu.VMEM((n_exp, 1), F32)],
        compiler_params=_cparams(("arbitrary", "arbitrary"), VMEM_LIMIT),
        name="route_plan",
    )(logits_t, router_b.reshape(n_exp, 1).astype(F32), tri, low)
    return gates, pos3, be, nu, nb


def _row_dma_loop(n_rows, make_copies):
    def body(it, carry):
        for u in range(DMA_UNROLL):
            for k, cp in enumerate(make_copies(it * DMA_UNROLL + u)):
                cp.start(priority=k % 2)
        return carry
    lax.fori_loop(0, n_rows // DMA_UNROLL, body, 0)


def _to_tile_rows(dst_ref, mat):
    m = mat.shape[0]
    for s in range(mat.shape[1] // LANES):
        dst_ref[pl.ds(s, m, stride=8), :] = mat[:, s * LANES:(s + 1) * LANES]


def _tile_row_chunk(src_ref, s, m):
    return src_ref[pl.ds(s, m, stride=8), :]


def _dispatch_kernel(pos_ref, hp_ref, xs_in_ref, xs_ref, sbuf, sem, *, n_tiles):
    del xs_in_ref
    i = pl.program_id(0)
    slot = i % 2

    def wait_slot(s):
        for _ in range(TOP_K):
            pltpu.make_async_copy(sbuf.at[s], xs_ref.at[pl.ds(0, ROW_TILE * 8)], sem.at[s]).wait()

    @pl.when(i >= 2)
    def _():
        wait_slot(slot)

    _to_tile_rows(sbuf.at[slot], hp_ref[...])

    def copies(r):
        return [pltpu.make_async_copy(
            sbuf.at[slot, pl.ds(pl.multiple_of(r * 8, 8), 8)],
            xs_ref.at[pl.ds(pl.multiple_of(pos_ref[0, 0, k * ROW_TILE + r] * 8, 8), 8)], sem.at[slot])
            for k in range(TOP_K)]

    _row_dma_loop(ROW_TILE, copies)

    @pl.when(i == n_tiles - 1)
    def _():
        wait_slot(slot)
        if n_tiles >= 2:
            wait_slot(1 - slot)


def _dispatch(hp, pos3, n_slots):
    t, half = hp.shape
    assert half == 8 * LANES, "a packed row must be exactly one (8, 128) tile"
    n_tiles = t // ROW_TILE
    xs0 = jnp.zeros((n_slots * 8, LANES), I32)
    return pl.pallas_call(
        functools.partial(_dispatch_kernel, n_tiles=n_tiles),
        grid=(n_tiles,),
        in_specs=[pl.BlockSpec((1, 1, TOP_K * ROW_TILE), lambda i: (i, 0, 0), memory_space=pltpu.SMEM),
                  pl.BlockSpec((ROW_TILE, half), lambda i: (i, 0)),
                  pl.BlockSpec(memory_space=pl.ANY)],
        out_specs=pl.BlockSpec(memory_space=pl.ANY),
        out_shape=jax.ShapeDtypeStruct((n_slots * 8, LANES), I32),
        scratch_shapes=[pltpu.VMEM((2, ROW_TILE * 8, LANES), I32), pltpu.SemaphoreType.DMA((2,))],
        input_output_aliases={2: 0},
        compiler_params=_cparams(("arbitrary",), VMEM_LIMIT),
        name="moe_dispatch",
    )(pos3, hp, xs0)


def _moe_kernel(be_ref, nu_ref, x_ref, wg_ref, wu_ref, wd_ref, y_ref):
    b = pl.program_id(0)
    half = wg_ref.shape[1] // 2
    n_used = nu_ref[0, 0]

    @pl.when(b < n_used)
    def _():
        u = jnp.concatenate([_tile_row_chunk(x_ref, s, MOE_BLK) for s in range(half // LANES)], axis=1)
        x_hi = _unpack_hi(u).astype(BF16)
        x_lo = _unpack_lo(u).astype(BF16)

        def proj(w_ref):
            return (jnp.dot(x_hi, w_ref[0, :half, :], preferred_element_type=F32)
                    + jnp.dot(x_lo, w_ref[0, half:, :], preferred_element_type=F32))

        act = (_silu(proj(wg_ref)) * proj(wu_ref)).astype(BF16)
        _to_tile_rows(y_ref, _pack_pairs(jnp.dot(act, wd_ref[0], preferred_element_type=F32)))

    @pl.when(b >= n_used)
    def _():
        y_ref[...] = jnp.zeros(y_ref.shape, y_ref.dtype)


def _moe(xs, be, n_used, wg, wu, wd):
    nb = xs.shape[0] // (MOE_BLK * 8)
    _, d, d_exp = wg.shape
    w_idx = lambda b, be, nu: (be[0, b], 0, 0)
    row_blk = pl.BlockSpec((MOE_BLK * 8, LANES), lambda b, be, nu: (b, 0))
    grid_spec = pltpu.PrefetchScalarGridSpec(
        num_scalar_prefetch=2,
        grid=(nb,),
        in_specs=[row_blk,
                  pl.BlockSpec((1, d, d_exp), w_idx),
                  pl.BlockSpec((1, d, d_exp), w_idx),
                  pl.BlockSpec((1, d_exp, d), w_idx)],
        out_specs=row_blk)
    return pl.pallas_call(
        _moe_kernel,
        grid_spec=grid_spec,
        out_shape=jax.ShapeDtypeStruct(xs.shape, I32),
        compiler_params=_cparams(("arbitrary",), VMEM_LIMIT),
        name="moe_experts",
    )(be, n_used, xs, wg, wu, wd)


def _combine_kernel(pos_ref, posn_ref, x_ref, mod_ref, gt_ref, y_hbm, o_ref, ybuf, sem, *, d, n_tiles):
    i = pl.program_id(0)
    slot = i % 2
    half = d // 2

    def issue(p_ref, s):
        def copies(r):
            return [pltpu.make_async_copy(
                y_hbm.at[pl.ds(pl.multiple_of(p_ref[0, 0, k * ROW_TILE + r] * 8, 8), 8)],
                ybuf.at[s, k, pl.ds(pl.multiple_of(r * 8, 8), 8)], sem.at[s])
                for k in range(TOP_K)]
        _row_dma_loop(ROW_TILE, copies)

    @pl.when(i == 0)
    def _():
        issue(pos_ref, 0)

    @pl.when(i + 1 < n_tiles)
    def _():
        issue(posn_ref, 1 - slot)

    for k in range(TOP_K):
        pltpu.make_async_copy(y_hbm.at[pl.ds(0, ROW_TILE * 8)], ybuf.at[slot, k], sem.at[slot]).wait()

    w0, w1 = gt_ref[:, 0:1], gt_ref[:, 1:2]
    for s in range(half // LANES):
        u0 = _tile_row_chunk(ybuf.at[slot, 0], s, ROW_TILE)
        u1 = _tile_row_chunk(ybuf.at[slot, 1], s, ROW_TILE)
        for off, unpack in ((0, _unpack_hi), (half, _unpack_lo)):
            lo, hi = off + s * LANES, off + (s + 1) * LANES
            f = w0 * unpack(u0) + w1 * unpack(u1)
            o_ref[:, lo:hi] = x_ref[:, lo:hi] + mod_ref[0, :, 5 * d + lo:5 * d + hi] * f


def _combine(x_mid, mod3, layer, gates, pos3, y, *, n_x_tiles, tiles_per_seq):
    t, d = x_mid.shape
    n_batch = n_x_tiles // tiles_per_seq
    n_tiles = t // ROW_TILE
    half = d // 2

    def mod_idx(i):
        return (layer * 8 + jnp.where(i < n_x_tiles, i // tiles_per_seq, n_batch), 0, 0)

    smem_blk = lambda f: pl.BlockSpec((1, 1, TOP_K * ROW_TILE), f, memory_space=pltpu.SMEM)
    return pl.pallas_call(
        functools.partial(_combine_kernel, d=d, n_tiles=n_tiles),
        grid=(n_tiles,),
        in_specs=[smem_blk(lambda i: (i, 0, 0)),
                  smem_blk(lambda i: (jnp.minimum(i + 1, n_tiles - 1), 0, 0)),
                  pl.BlockSpec((ROW_TILE, d), lambda i: (i, 0)),
                  pl.BlockSpec((1, 1, mod3.shape[2]), mod_idx),
                  pl.BlockSpec((ROW_TILE, TOP_K), lambda i: (i, 0)),
                  pl.BlockSpec(memory_space=pl.ANY)],
        out_specs=pl.BlockSpec((ROW_TILE, d), lambda i: (i, 0)),
        out_shape=jax.ShapeDtypeStruct((t, d), F32),
        scratch_shapes=[pltpu.VMEM((2, TOP_K, ROW_TILE * 8, LANES), I32), pltpu.SemaphoreType.DMA((2,))],
        compiler_params=_cparams(("arbitrary",), VMEM_LIMIT),
        name="moe_combine",
    )(pos3, pos3, x_mid, mod3, gates, y)


def _rope_tables(seq, tm):
    nf = HEAD_DIM // 4
    inv = np.power(np.float32(ROPE_BASE), -np.arange(nf, dtype=np.float32) / np.float32(nf)).astype(np.float32)
    tt = np.arange(seq)
    ar = (tt // GRID_W).astype(np.float32)[:, None] * inv
    ac = (tt % GRID_W).astype(np.float32)[:, None] * inv
    ang = np.concatenate([ar, ar, ac, ac], axis=-1)
    cos = np.concatenate([np.cos(ang), np.ones((tm, HEAD_DIM), np.float32)], axis=0)
    sin = np.concatenate([np.sin(ang), np.zeros((tm, HEAD_DIM), np.float32)], axis=0)
    return jnp.asarray(cos, F32), jnp.asarray(sin, F32)


def kernel(x, c, ctx, c_ctx, ada_w, ada_b, norm_mix_g, norm_ffn_g, ab_w_in, ab_w_out, ab_q_norm, ab_k_norm, ab_sink, conv_w, conv_b, conv_ln_g, conv_ln_b, na_w_in, na_w_out, na_q_norm, na_k_norm, na_rpb, router_w, router_b, moe_w_gate, moe_w_up, moe_w_down):
    batch, seq, d = x.shape
    ctx_len = ctx.shape[1]
    depth = ada_w.shape[0]
    n_exp = router_w.shape[1]
    b_ch = conv_w.shape[-1]
    a_qw = ab_w_out.shape[1] - b_ch
    a_kvw = (ab_w_in.shape[-1] - a_qw - 2 * b_ch) // 2
    n_kv = 2
    na_w = na_w_out.shape[1]
    assert batch + 1 <= 8 and a_qw == b_ch and n_exp % 8 == 0

    tx, tc = batch * seq, batch * ctx_len
    t = tx + tc
    tm_in = next(m for m in (1024, 512, 256) if seq % m == 0 and tc % m == 0)
    tn = 512 if d >= 2048 else 256

    c8 = jnp.zeros((8, d), F32).at[:batch].set(c).at[batch].set(c_ctx)
    mod3 = _ada_all(c8, ada_w, ada_b).reshape(depth * 8, 1, 6 * d)

    cos, sin = _rope_tables(seq, tm_in)
    rw = jnp.zeros((d, LANES), F32).at[:, :n_exp].set(router_w.astype(F32))
    rw_hi = rw.astype(BF16)
    rw_cat = jnp.concatenate([rw_hi, (rw - rw_hi.astype(F32)).astype(BF16)], axis=1)
    rows = seq // GRID_W

    k0, v0, u0 = a_qw, a_qw + a_kvw, a_qw + 2 * a_kvw
    k_off, v_off = a_qw + 2 * b_ch, a_qw + 2 * b_ch + a_kvw
    ab_w_in_b = jnp.concatenate([ab_w_in[..., :k0], ab_w_in[..., u0:], ab_w_in[..., k0:u0]], axis=-1).astype(BF16)
    ab_w_out_b = ab_w_out.astype(BF16)
    na_w_in_b = na_w_in.astype(BF16)
    na_w_out_b = na_w_out.astype(BF16)
    d_exp = moe_w_gate.shape[-1]
    wg_all = moe_w_gate.astype(BF16).reshape(depth * n_exp, d, d_exp)
    wu_all = moe_w_up.astype(BF16).reshape(depth * n_exp, d, d_exp)
    wd_all = moe_w_down.astype(BF16).reshape(depth * n_exp, d_exp, d)

    def ab_kind(col):
        return "q" if col < a_qw else ("k" if k_off <= col < v_off else "p")

    def na_kind(col):
        return "q" if col < na_w else ("k" if col < 2 * na_w else "p")

    tok = jnp.concatenate([x.reshape(tx, d), ctx.reshape(tc, d)], axis=0)

    for i in range(depth):
        with_ctx = i < depth - 1
        j = i // 2
        t_act = t if with_ctx else tx
        tiles = dict(n_x_tiles=tx // tm_in, tiles_per_seq=seq // tm_in)
        if i % 2 == 0:
            px = _inproj(tok, mod3, i, norm_mix_g[i], ab_w_in_b, j, ab_q_norm[j], ab_k_norm[j], ab_kind, cos, sin,
                         tm=tm_in, tn=tn, **tiles)
            att = _win_attn(px, ab_sink[j], batch=batch, seq=seq, ctx_len=ctx_len, a_qw=a_qw, n_kv=n_kv,
                            k_off=k_off, v_off=v_off, with_ctx=with_ctx)
            cv = _conv(px, conv_w[j], conv_b[j], conv_ln_g[j], conv_ln_b[j], a_off=a_qw, g_off=a_qw + b_ch,
                       batch=batch, seq=seq, ctx_len=ctx_len, with_ctx=with_ctx, tmc=256)
            acts, w_out = [att, cv], ab_w_out_b
        else:
            px = _inproj(tok, mod3, i, norm_mix_g[i], na_w_in_b, j, na_q_norm[j], na_k_norm[j], na_kind, None, None,
                         tm=tm_in, tn=tn, **tiles)
            bias = _na_bias_table(na_rpb[j], rows)
            att = _na_attn(px, bias, batch=batch, seq=seq, ctx_len=ctx_len, n_heads=na_w // HEAD_DIM,
                           with_ctx=with_ctx)
            acts, w_out = [att], na_w_out_b

        otiles = dict(n_x_tiles=tx // ROW_TILE, tiles_per_seq=seq // ROW_TILE)
        x_mid, hp, logits_t = _outproj(tok, mod3, i, norm_ffn_g[i], rw_cat, rw_hi, acts, w_out, j, t_act=t_act,
                                       n_exp=n_exp, tm=ROW_TILE, **otiles)
        g_out, pos3, be, n_used, nb = _route_plan(logits_t, router_b, i * n_exp)
        xs = _dispatch(hp, pos3, nb * MOE_BLK)
        y = _moe(xs, be, n_used, wg_all, wu_all, wd_all)
        tok = _combine(x_mid, mod3, i, g_out[:TOP_K].T, pos3, y, **otiles)

    return tok[:tx].reshape(batch, seq, d)
```

```python
import functools

import numpy as np
import jax
import jax.numpy as jnp
from jax import lax
from jax.experimental import pallas as pl
from jax.experimental.pallas import tpu as pltpu

F32 = jnp.float32
BF16 = jnp.bfloat16
I32 = jnp.int32

HEAD_DIM = 128
LANES = 128
GRID_W = 64
WINDOW = 128
N_GROUPS = 4
TOP_K = 2
ROPE_BASE = 10000.0
EPS = 1e-6
NEG_INF = -1e30
MOE_BLK = 256
ROW_TILE = 256
NA_RB = 4
NA_HG = 8
HALO = 16
DMA_UNROLL = 64
VMEM_LIMIT = 56 * 1024 * 1024
HI_MASK = -65536
LOG2E = 1.4426950408889634
Q_SCALE = HEAD_DIM ** -0.5 * LOG2E


def _cparams(sem, vmem=None):
    return pltpu.CompilerParams(dimension_semantics=sem, vmem_limit_bytes=vmem)


def _silu(v):
    return v * jax.nn.sigmoid(v)


def _rms(v, g):
    ms = jnp.mean(v * v, axis=-1, keepdims=True)
    return v * lax.rsqrt(ms + EPS) * g


def _pack_pairs(v):
    half = v.shape[1] // 2
    bits = pltpu.bitcast(v.astype(BF16).astype(F32), I32)
    return (bits[:, :half] & HI_MASK) | lax.shift_right_logical(bits[:, half:], 16)


def _unpack_hi(u):
    return pltpu.bitcast(u & HI_MASK, F32)


def _unpack_lo(u):
    return pltpu.bitcast(lax.shift_left(u, 16), F32)


def _ada_kernel(c_ref, w_ref, b_ref, o_ref):
    sc = _silu(c_ref[...])
    o_ref[0] = jnp.dot(sc.astype(BF16), w_ref[0].astype(BF16), preferred_element_type=F32) + b_ref[0]


def _ada_all(c8, ada_w, ada_b):
    depth, d, n = ada_w.shape
    tn = min(n, 1024)
    return pl.pallas_call(
        _ada_kernel,
        grid=(depth, n // tn),
        in_specs=[pl.BlockSpec((8, d), lambda l, j: (0, 0)),
                  pl.BlockSpec((1, d, tn), lambda l, j: (l, 0, j)),
                  pl.BlockSpec((1, 1, tn), lambda l, j: (l, 0, j))],
        out_specs=pl.BlockSpec((1, 8, tn), lambda l, j: (l, 0, j)),
        out_shape=jax.ShapeDtypeStruct((depth, 8, n), F32),
        compiler_params=_cparams(("arbitrary", "arbitrary"), VMEM_LIMIT),
        name="ada_mod",
    )(c8, ada_w, ada_b.reshape(depth, 1, n))


def _rope(y, cos, sin):
    lane = lax.broadcasted_iota(I32, y.shape, 1)
    first = (lane & 32) == 0
    fwd = pltpu.roll(y, 32, 1)
    bwd = pltpu.roll(y, 96, 1)
    return y * cos + jnp.where(first, -bwd, fwd) * sin


def _inproj_kernel(*refs, d, tn, groups, rope):
    if rope:
        x_ref, mod_ref, g_ref, w_ref, qg_ref, kg_ref, cos_ref, sin_ref, o_ref, h_scr = refs
    else:
        x_ref, mod_ref, g_ref, w_ref, qg_ref, kg_ref, o_ref, h_scr = refs
        cos_ref = sin_ref = None
    j = pl.program_id(1)

    @pl.when(j == 0)
    def _():
        y = _rms(x_ref[...], g_ref[...])
        shift = mod_ref[0, :, 0:d]
        scale = mod_ref[0, :, d:2 * d]
        h_scr[...] = (y * (1.0 + scale) + shift).astype(BF16)

    acc = jnp.dot(h_scr[...], w_ref[...], preferred_element_type=F32)

    for kinds, lo, hi in groups:
        @pl.when((j >= lo) & (j <= hi))
        def _(kinds=kinds):
            if all(k == "p" for k in kinds):
                o_ref[...] = acc.astype(o_ref.dtype)
                return
            for s, kind in enumerate(kinds):
                piece = acc[:, s * LANES:(s + 1) * LANES]
                if kind != "p":
                    piece = _rms(piece, (qg_ref if kind == "q" else kg_ref)[...])
                    if rope:
                        piece = _rope(piece, cos_ref[...], sin_ref[...])
                    if kind == "q":
                        piece = piece * Q_SCALE
                o_ref[:, s * LANES:(s + 1) * LANES] = piece.astype(o_ref.dtype)


def _inproj(tok, mod3, layer, norm_g, w, w_layer, qg, kg, kind_of_col, cos, sin, *, n_x_tiles, tiles_per_seq, tm, tn):
    t, d = tok.shape
    n = w.shape[2]
    nj = n // tn
    per_j = [tuple(kind_of_col(j * tn + s * LANES) for s in range(tn // LANES)) for j in range(nj)]
    groups = []
    for j, kinds in enumerate(per_j):
        if groups and groups[-1][0] == kinds and groups[-1][2] == j - 1:
            groups[-1] = (kinds, groups[-1][1], j)
        else:
            groups.append((kinds, j, j))
    rope = cos is not None
    n_batch = n_x_tiles // tiles_per_seq

    def mod_idx(i, j):
        return (layer * 8 + jnp.where(i < n_x_tiles, i // tiles_per_seq, n_batch), 0, 0)

    def pos_idx(i, j):
        return (jnp.where(i < n_x_tiles, i % tiles_per_seq, tiles_per_seq), 0)

    in_specs = [pl.BlockSpec((tm, d), lambda i, j: (i, 0)),
                pl.BlockSpec((1, 1, mod3.shape[2]), mod_idx),
                pl.BlockSpec((1, d), lambda i, j: (0, 0)),
                pl.BlockSpec((None, d, tn), lambda i, j: (w_layer, 0, j)),
                pl.BlockSpec((1, HEAD_DIM), lambda i, j: (0, 0)),
                pl.BlockSpec((1, HEAD_DIM), lambda i, j: (0, 0))]
    args = [tok, mod3, norm_g.reshape(1, d), w, qg.reshape(1, HEAD_DIM), kg.reshape(1, HEAD_DIM)]
    if rope:
        in_specs += [pl.BlockSpec((tm, HEAD_DIM), pos_idx), pl.BlockSpec((tm, HEAD_DIM), pos_idx)]
        args += [cos, sin]
    return pl.pallas_call(
        functools.partial(_inproj_kernel, d=d, tn=tn, groups=tuple(groups), rope=rope),
        grid=(t // tm, nj),
        in_specs=in_specs,
        out_specs=pl.BlockSpec((tm, tn), lambda i, j: (i, j)),
        out_shape=jax.ShapeDtypeStruct((t, n), BF16),
        scratch_shapes=[pltpu.VMEM((tm, d), BF16)],
        compiler_params=_cparams(("arbitrary", "arbitrary"), VMEM_LIMIT),
        name="in_proj",
    )(*args)


def _softmax_pv(s, v, extra_logit=None):
    m = jnp.max(s, axis=-1, keepdims=True)
    if extra_logit is not None:
        m = jnp.maximum(m, extra_logit)
    p = jnp.exp2(s - m)
    den = jnp.sum(p, axis=-1, keepdims=True)
    if extra_logit is not None:
        den = den + jnp.exp2(extra_logit - m)
    o = jnp.dot(p.astype(BF16), v, preferred_element_type=F32)
    return o / den


def _win_attn_kernel(sink_ref, q_ref, kp_ref, kc_ref, kn_ref, vp_ref, vc_ref, vn_ref, kx_ref, vx_ref, o_ref,
                     *, n_grp, n_kv, nb):
    n = pl.program_id(1)
    w = WINDOW
    is_x = n < nb
    lo = jnp.where(is_x, jnp.where(n > 0, 0, w), 0)
    hi = jnp.where(is_x, jnp.where(n < nb - 1, 3 * w, 2 * w), 0)
    for h in range(n_kv):
        kl = slice(h * HEAD_DIM, (h + 1) * HEAD_DIM)
        qs = jnp.concatenate([q_ref[:, (h * n_grp + g) * HEAD_DIM:(h * n_grp + g + 1) * HEAD_DIM]
                              for g in range(n_grp)], axis=0)
        k = jnp.concatenate([kp_ref[:, kl], kc_ref[:, kl], kn_ref[:, kl], kx_ref[:, kl]], axis=0)
        v = jnp.concatenate([vp_ref[:, kl], vc_ref[:, kl], vn_ref[:, kl], vx_ref[:, kl]], axis=0)
        s = lax.dot_general(qs, k, (((1,), (1,)), ((), ())), preferred_element_type=F32)
        rows = lax.broadcasted_iota(I32, s.shape, 0) & (w - 1)
        cols = lax.broadcasted_iota(I32, s.shape, 1)
        local_ok = (jnp.abs(cols - w - rows) <= WINDOW) & (cols >= lo) & (cols < hi)
        s = jnp.where(local_ok | (cols >= 3 * w), s, NEG_INF)
        for g in range(n_grp):
            o = _softmax_pv(s[g * w:(g + 1) * w], v, sink_ref[h, g] * LOG2E)
            col = (h * n_grp + g) * HEAD_DIM
            o_ref[:, col:col + HEAD_DIM] = o.astype(o_ref.dtype)


def _win_attn(px, sink, *, batch, seq, ctx_len, a_qw, n_kv, k_off, v_off, with_ctx):
    t = px.shape[0]
    w = WINDOW
    n_grp = a_qw // HEAD_DIM // n_kv
    nb = seq // w
    nq = nb + (ctx_len // w if with_ctx else 0)
    kvw = n_kv * HEAD_DIM
    assert k_off % kvw == 0 and v_off % kvw == 0
    ctx_blk0 = batch * seq // ctx_len

    def q_idx(b, n):
        return (jnp.where(n < nb, b * nb + n, batch * nb + b * (ctx_len // w) + (n - nb)), 0)

    def kv_idx(off, delta):
        def f(b, n):
            return (b * nb + jnp.clip(n + delta, 0, nb - 1), off // kvw)
        return f

    def ctx_idx(off):
        return lambda b, n: (ctx_blk0 + b, off // kvw)

    blk = lambda f: pl.BlockSpec((w, kvw), f)
    in_specs = [pl.BlockSpec(memory_space=pltpu.SMEM),
                pl.BlockSpec((w, a_qw), q_idx),
                blk(kv_idx(k_off, -1)), blk(kv_idx(k_off, 0)), blk(kv_idx(k_off, 1)),
                blk(kv_idx(v_off, -1)), blk(kv_idx(v_off, 0)), blk(kv_idx(v_off, 1)),
                pl.BlockSpec((ctx_len, kvw), ctx_idx(k_off)),
                pl.BlockSpec((ctx_len, kvw), ctx_idx(v_off))]
    return pl.pallas_call(
        functools.partial(_win_attn_kernel, n_grp=n_grp, n_kv=n_kv, nb=nb),
        grid=(batch, nq),
        in_specs=in_specs,
        out_specs=pl.BlockSpec((w, a_qw), q_idx),
        out_shape=jax.ShapeDtypeStruct((t if with_ctx else batch * seq, a_qw), BF16),
        compiler_params=_cparams(("arbitrary",) * 2, VMEM_LIMIT),
        name="win_attn",
    )(sink.reshape(n_kv, n_grp).astype(F32), *([px] * 9))


def _na_attn_kernel(q_ref, kp_ref, kc_ref, kn_ref, vp_ref, vc_ref, vn_ref, kx_ref, vx_ref, bias_ref, o_ref,
                    *, n_hg):
    nloc = 3 * NA_RB * GRID_W
    for hh in range(n_hg):
        cl = slice(hh * HEAD_DIM, (hh + 1) * HEAD_DIM)
        k = jnp.concatenate([kp_ref[:, cl], kc_ref[:, cl], kn_ref[:, cl], kx_ref[:, cl]], axis=0)
        v = jnp.concatenate([vp_ref[:, cl], vc_ref[:, cl], vn_ref[:, cl], vx_ref[:, cl]], axis=0)
        s = lax.dot_general(q_ref[:, cl], k, (((1,), (1,)), ((), ())), preferred_element_type=F32)
        s = jnp.concatenate([s[:, :nloc] + bias_ref[hh, 0], s[:, nloc:]], axis=1)
        o_ref[:, cl] = _softmax_pv(s, v).astype(o_ref.dtype)


def _na_bias_table(rpb, rows):
    n_heads, n_dr, n_dc = rpb.shape
    kh, kw = (n_dr + 1) // 2, (n_dc + 1) // 2
    n_rb = rows // NA_RB
    cidx = np.arange(GRID_W)
    cs = np.clip(cidx - kw // 2, 0, GRID_W - kw)
    col_ok = (cidx[None, :] >= cs[:, None]) & (cidx[None, :] < cs[:, None] + kw)
    dc_idx = np.clip(cidx[None, :] - cidx[:, None], -(kw - 1), kw - 1) + kw - 1
    a = jnp.where(col_ok[None, None], rpb.astype(F32)[:, :, dc_idx] * LOG2E, NEG_INF)
    masked = jnp.full((n_heads, GRID_W, GRID_W), NEG_INF, F32)
    classes = []
    for rb in (0, min(1, n_rb - 1), n_rb - 1):
        qrows = []
        for j in range(NA_RB):
            r = rb * NA_RB + j
            rs = int(np.clip(r - kh // 2, 0, rows - kh))
            blocks = []
            for tblk in range(3):
                for krl in range(NA_RB):
                    kr = (rb - 1 + tblk) * NA_RB + krl
                    ok = (rs <= kr < rs + kh) and (0 <= kr < rows)
                    blocks.append(a[:, kr - r + kh - 1] if ok else masked)
            qrows.append(jnp.concatenate(blocks, axis=-1))
        classes.append(jnp.concatenate(qrows, axis=1))
    classes.append(jnp.full_like(classes[0], NEG_INF))
    return jnp.stack(classes, axis=1)


def _na_attn(px, bias, *, batch, seq, ctx_len, n_heads, with_ctx):
    t = px.shape[0]
    qb = NA_RB * GRID_W
    assert ctx_len == qb, "context queries are processed as one extra query block"
    n_rb = seq // qb
    nq = n_rb + (1 if with_ctx else 0)
    na_w = n_heads * HEAD_DIM
    n_hg = min(NA_HG, n_heads)
    gw = n_hg * HEAD_DIM
    assert n_heads % n_hg == 0
    ctx_blk0 = batch * seq // ctx_len

    def q_idx(b, h, r):
        return (jnp.where(r < n_rb, b * n_rb + r, batch * n_rb + b), h)

    def kv_idx(off, delta):
        return lambda b, h, r: (b * n_rb + jnp.clip(r + delta, 0, n_rb - 1), off // gw + h)

    def ctx_idx(off):
        return lambda b, h, r: (ctx_blk0 + b, off // gw + h)

    def bias_idx(b, h, r):
        return (h, jnp.where(r == 0, 0, jnp.where(r < n_rb - 1, 1, jnp.where(r == n_rb - 1, 2, 3))), 0, 0)

    blk = lambda f: pl.BlockSpec((qb, gw), f)
    in_specs = [blk(q_idx),
                blk(kv_idx(na_w, -1)), blk(kv_idx(na_w, 0)), blk(kv_idx(na_w, 1)),
                blk(kv_idx(2 * na_w, -1)), blk(kv_idx(2 * na_w, 0)), blk(kv_idx(2 * na_w, 1)),
                pl.BlockSpec((ctx_len, gw), ctx_idx(na_w)),
                pl.BlockSpec((ctx_len, gw), ctx_idx(2 * na_w)),
                pl.BlockSpec((n_hg, 1, qb, 3 * qb), bias_idx)]
    return pl.pallas_call(
        functools.partial(_na_attn_kernel, n_hg=n_hg),
        grid=(batch, n_heads // n_hg, nq),
        in_specs=in_specs,
        out_specs=blk(q_idx),
        out_shape=jax.ShapeDtypeStruct((t if with_ctx else batch * seq, na_w), BF16),
        compiler_params=_cparams(("arbitrary",) * 3, VMEM_LIMIT),
        name="na_attn",
    )(*([px] * 9), bias)


def _conv_kernel(a_ref, g_ref, ap_ref, gp_ref, an_ref, gn_ref, w_ref, b_ref, lg_ref, lb_ref, o_ref, hbuf, hs, cbuf,
                 *, tiles_per_seq, n_x_tiles, n_taps, tmc, sub):
    i = pl.program_id(0)
    p = i % tiles_per_seq
    is_x = i < n_x_tiles
    has_prev = is_x & (p > 0)
    has_next = is_x & (p < tiles_per_seq - 1)

    def glu(a, g):
        return a.astype(F32) * jax.nn.sigmoid(g.astype(F32))

    hbuf[0:HALO, :] = jnp.where(has_prev, glu(ap_ref[...], gp_ref[...]), 0.0)
    hbuf[HALO:HALO + tmc, :] = glu(a_ref[...], g_ref[...])
    hbuf[HALO + tmc:2 * HALO + tmc, :] = jnp.where(has_next, glu(an_ref[...], gn_ref[...]), 0.0)

    n_buf = tmc + 2 * HALO
    for sh in range(1, 8):
        hs[sh - 1, 0:n_buf - 8, :] = hbuf[sh:sh + n_buf - 8, :]

    ch = a_ref.shape[1]
    first = HALO - n_taps // 2
    for c in range(ch // LANES):
        cl = slice(c * LANES, (c + 1) * LANES)
        for tb in range(tmc // sub):
            acc = jnp.zeros((sub, LANES), F32)
            for k in range(n_taps):
                sh = (first + k) % 8
                r0 = tb * sub + first + k - sh
                slab = hbuf[r0:r0 + sub, cl] if sh == 0 else hs[sh - 1, r0:r0 + sub, cl]
                acc = acc + slab * w_ref[k:k + 1, cl]
            cbuf[tb * sub:(tb + 1) * sub, cl] = acc + b_ref[:, cl]

    y = cbuf[...]
    mu = jnp.mean(y, axis=-1, keepdims=True)
    yc = y - mu
    var = jnp.mean(yc * yc, axis=-1, keepdims=True)
    yn = yc * lax.rsqrt(var + EPS) * lg_ref[...] + lb_ref[...]
    o_ref[...] = _silu(yn).astype(o_ref.dtype)


def _conv(px, conv_w, conv_b, ln_g, ln_b, *, a_off, g_off, batch, seq, ctx_len, with_ctx, tmc):
    t = px.shape[0]
    n_taps, ch = conv_w.shape
    assert n_taps // 2 <= HALO and ctx_len == tmc and seq % tmc == 0
    tiles_per_seq = seq // tmc
    n_x_tiles = batch * tiles_per_seq
    n_tiles = n_x_tiles + (batch if with_ctx else 0)
    hpt = tmc // HALO
    n_hblk = t // HALO
    w_pad = jnp.zeros((32, ch), F32).at[:n_taps].set(conv_w.astype(F32))

    main = lambda off: pl.BlockSpec((tmc, ch), lambda i: (i, off // ch))
    prev = lambda off: pl.BlockSpec((HALO, ch), lambda i: (jnp.maximum(i * hpt - 1, 0), off // ch))
    nxt = lambda off: pl.BlockSpec((HALO, ch), lambda i: (jnp.minimum((i + 1) * hpt, n_hblk - 1), off // ch))
    vec = lambda: pl.BlockSpec((1, ch), lambda i: (0, 0))
    return pl.pallas_call(
        functools.partial(_conv_kernel, tiles_per_seq=tiles_per_seq, n_x_tiles=n_x_tiles, n_taps=n_taps,
                          tmc=tmc, sub=64),
        grid=(n_tiles,),
        in_specs=[main(a_off), main(g_off), prev(a_off), prev(g_off), nxt(a_off), nxt(g_off),
                  pl.BlockSpec((32, ch), lambda i: (0, 0)), vec(), vec(), vec()],
        out_specs=pl.BlockSpec((tmc, ch), lambda i: (i, 0)),
        out_shape=jax.ShapeDtypeStruct((t if with_ctx else batch * seq, ch), BF16),
        scratch_shapes=[pltpu.VMEM((tmc + 2 * HALO, ch), F32), pltpu.VMEM((7, tmc + 2 * HALO, ch), F32),
                        pltpu.VMEM((tmc, ch), F32)],
        compiler_params=_cparams(("arbitrary",), VMEM_LIMIT),
        name="conformer_conv",
    )(px, px, px, px, px, px, w_pad, conv_b.reshape(1, ch).astype(F32), ln_g.reshape(1, ch).astype(F32),
      ln_b.reshape(1, ch).astype(F32))


def _outproj_kernel(*refs, d, n_in, n_exp):
    x_ref, mod_ref, g_ref, rwc_ref, rwh_ref = refs[:5]
    a_refs = refs[5:5 + n_in]
    w_refs = refs[5 + n_in:5 + 2 * n_in]
    xo_ref, hp_ref, lt_ref = refs[5 + 2 * n_in:]
    o = jnp.dot(a_refs[0][...], w_refs[0][...], preferred_element_type=F32)
    for a_ref, w_ref in zip(a_refs[1:], w_refs[1:]):
        o = o + jnp.dot(a_ref[...], w_ref[...], preferred_element_type=F32)
    x_new = x_ref[...] + mod_ref[0, :, 2 * d:3 * d] * o
    xo_ref[...] = x_new
    hf = _rms(x_new, g_ref[...]) * (1.0 + mod_ref[0, :, 4 * d:5 * d]) + mod_ref[0, :, 3 * d:4 * d]
    hp_ref[...] = _pack_pairs(hf)
    h_hi = hf.astype(BF16)
    h_lo = (hf - h_hi.astype(F32)).astype(BF16)
    both = jnp.dot(h_hi, rwc_ref[...], preferred_element_type=F32)
    lg = both[:, :LANES] + both[:, LANES:] + jnp.dot(h_lo, rwh_ref[...], preferred_element_type=F32)
    lt_ref[...] = lg.T[:n_exp]


def _outproj(tok, mod3, layer, norm_g, rw_cat, rw_hi, acts, w, w_layer, *, t_act, n_exp, n_x_tiles, tiles_per_seq, tm):
    d = tok.shape[1]
    n_batch = n_x_tiles // tiles_per_seq
    n_in = len(acts)

    def mod_idx(i):
        return (layer * 8 + jnp.where(i < n_x_tiles, i // tiles_per_seq, n_batch), 0, 0)

    in_specs = [pl.BlockSpec((tm, d), lambda i: (i, 0)),
                pl.BlockSpec((1, 1, mod3.shape[2]), mod_idx),
                pl.BlockSpec((1, d), lambda i: (0, 0)),
                pl.BlockSpec(rw_cat.shape, lambda i: (0, 0)),
                pl.BlockSpec(rw_hi.shape, lambda i: (0, 0))]
    in_specs += [pl.BlockSpec((tm, a.shape[1]), lambda i: (i, 0)) for a in acts]
    kw = acts[0].shape[1]
    assert all(a.shape[1] == kw for a in acts) and w.shape[1] == kw * n_in
    in_specs += [pl.BlockSpec((None, kw, d), lambda i, r=r: (w_layer, r, 0)) for r in range(n_in)]
    return pl.pallas_call(
        functools.partial(_outproj_kernel, d=d, n_in=n_in, n_exp=n_exp),
        grid=(t_act // tm,),
        in_specs=in_specs,
        out_specs=[pl.BlockSpec((tm, d), lambda i: (i, 0)),
                   pl.BlockSpec((tm, d // 2), lambda i: (i, 0)),
                   pl.BlockSpec((n_exp, tm), lambda i: (0, i))],
        out_shape=[jax.ShapeDtypeStruct((t_act, d), F32),
                   jax.ShapeDtypeStruct((t_act, d // 2), I32),
                   jax.ShapeDtypeStruct((n_exp, t_act), F32)],
        compiler_params=_cparams(("arbitrary",), VMEM_LIMIT),
        name="out_proj",
    )(tok, mod3, norm_g.reshape(1, d), rw_cat, rw_hi, *acts, *([w] * n_in))


def _route_tile(logits, bias, n_exp):
    epg = n_exp // N_GROUPS
    aff = jax.nn.sigmoid(logits)
    sel = aff + bias
    row = lambda a, i: a[i:i + 1, :]

    scores = []
    for g in range(N_GROUPS):
        best = None
        for i in range(epg):
            for j in range(i + 1, epg):
                pair = row(sel, g * epg + i) + row(sel, g * epg + j)
                best = pair if best is None else jnp.maximum(best, pair)
        scores.append(best)
    grp = jnp.zeros(scores[0].shape, I32)
    top = scores[0]
    for g in range(1, N_GROUPS):
        better = scores[g] > top
        grp = jnp.where(better, g, grp)
        top = jnp.where(better, scores[g], top)

    def pick(a, i):
        out = row(a, i)
        for g in range(1, N_GROUPS):
            out = jnp.where(grp == g, row(a, g * epg + i), out)
        return out

    v = [pick(sel, i) for i in range(epg)]
    a = [pick(aff, i) for i in range(epg)]
    ranks = []
    for i in range(epg):
        r = jnp.zeros(grp.shape, I32)
        for j in range(epg):
            if j != i:
                ahead = (v[j] > v[i]) | ((v[j] == v[i]) & (j < i)) if j < i else (v[j] > v[i])
                r = r + ahead.astype(I32)
        ranks.append(r)
    zero_i, zero_f = jnp.zeros(grp.shape, I32), jnp.zeros(grp.shape, F32)
    experts, affs = [], []
    for k in range(TOP_K):
        e_k, a_k = zero_i, zero_f
        for i in range(epg):
            hit = ranks[i] == k
            e_k = jnp.where(hit, i, e_k)
            a_k = jnp.where(hit, a[i], a_k)
        experts.append(grp * epg + e_k)
        affs.append(a_k)
    den = affs[0] + affs[1]
    return experts, [a_k / den for a_k in affs]


def _route_plan_kernel(l_ref, b_ref, tri_ref, low_ref, g_ref, pos_ref, be_ref, nu_ref, cnt_scr, carry_scr,
                       *, n_exp, tt, be_offset):
    ph = pl.program_id(0)
    i = pl.program_id(1)
    experts, gates = _route_tile(l_ref[...], b_ref[...], n_exp)
    eid = lax.broadcasted_iota(I32, (n_exp, tt), 0)
    onehot = [(eid == e_k).astype(F32) for e_k in experts]
    both = onehot[0] + onehot[1]
    tile_cnt = jnp.sum(both, axis=1, keepdims=True)

    @pl.when(ph == 0)
    def _():
        @pl.when(i == 0)
        def _():
            cnt_scr[...] = jnp.zeros(cnt_scr.shape, F32)
        cnt_scr[...] += tile_cnt

    @pl.when(ph == 1)
    def _():
        @pl.when(i == 0)
        def _():
            carry_scr[...] = jnp.zeros(carry_scr.shape, F32)
        blocks = jnp.floor((cnt_scr[...] + (MOE_BLK - 1)) * (1.0 / MOE_BLK))
        blocks_b = jnp.broadcast_to(blocks, (n_exp, LANES)).astype(BF16)
        first_blk = jnp.dot(low_ref[...], blocks_b, preferred_element_type=F32)[:, 0:1]
        prefix = jnp.dot(both.astype(BF16), tri_ref[...], preferred_element_type=F32)
        slot = first_blk * MOE_BLK + carry_scr[...] + prefix
        carry_scr[...] += tile_cnt
        pos = [jnp.sum(oh * slot, axis=0, keepdims=True).astype(I32) for oh in onehot]
        for q in range(tt // ROW_TILE):
            for k in range(TOP_K):
                pos_ref[q, :, k * ROW_TILE:(k + 1) * ROW_TILE] = pos[k][:, q * ROW_TILE:(q + 1) * ROW_TILE]
        g_ref[...] = jnp.concatenate(gates + [jnp.zeros_like(gates[0])] * (8 - TOP_K), axis=0)
        last_blk = first_blk + blocks
        bidx = lax.broadcasted_iota(I32, (n_exp, be_ref.shape[1]), 1).astype(F32)
        owner = jnp.sum((last_blk <= bidx).astype(F32), axis=0, keepdims=True)
        be_ref[...] = jnp.minimum(owner, n_exp - 1.0).astype(I32) + be_offset
        nu_ref[...] = jnp.broadcast_to(jnp.sum(blocks, axis=0, keepdims=True), nu_ref.shape).astype(I32)


def _route_plan(logits_t, router_b, be_offset):
    n_exp, t = logits_t.shape
    tt = next(m for m in (1024, 512, 256) if t % m == 0)
    n_tiles = t // tt
    nb = -(-(t * TOP_K + n_exp * (MOE_BLK - 1)) // MOE_BLK)
    nb_pad = -(-nb // LANES) * LANES
    tok_i = np.arange(tt)
    tri = jnp.asarray(tok_i[:, None] < tok_i[None, :], BF16)
    exp_i = np.arange(n_exp)
    low = jnp.asarray(exp_i[None, :] < exp_i[:, None], BF16)
    const = lambda shape: pl.BlockSpec(shape, lambda ph, i: (0,) * len(shape))
    gates, pos3, be, nu = pl.pallas_call(
        functools.partial(_route_plan_kernel, n_exp=n_exp, tt=tt, be_offset=be_offset),
        grid=(2, n_tiles),
        in_specs=[pl.BlockSpec((n_exp, tt), lambda ph, i: (0, i)),
                  const((n_exp, 1)), const((tt, tt)), const((n_exp, n_exp))],
        out_specs=[pl.BlockSpec((8, tt), lambda ph, i: (0, i * ph)),
                   pl.BlockSpec((tt // ROW_TILE, 1, TOP_K * ROW_TILE), lambda ph, i: (i * ph, 0, 0)),
                   const((1, nb_pad)), const((1, LANES))],
        out_shape=[jax.ShapeDtypeStruct((8, t), F32),
                   jax.ShapeDtypeStruct((t // ROW_TILE, 1, TOP_K * ROW_TILE), I32),
                   jax.ShapeDtypeStruct((1, nb_pad), I32),
                   jax.ShapeDtypeStruct((1, LANES), I32)],
        scratch_shapes=[pltpu.VMEM((n_exp, 1), F32), pltpu.VMEM((n_exp, 1), F32)],
        compiler_params=_cparams(("arbitrary", "arbitrary"), VMEM_LIMIT),
        name="route_plan",
    )(logits_t, router_b.reshape(n_exp, 1).astype(F32), tri, low)
    return gates, pos3, be, nu, nb


def _row_dma_loop(n_rows, make_copies):
    def body(it, carry):
        for u in range(DMA_UNROLL):
            for k, cp in enumerate(make_copies(it * DMA_UNROLL + u)):
                cp.start(priority=k % 2)
        return carry
    lax.fori_loop(0, n_rows // DMA_UNROLL, body, 0)


def _to_tile_rows(dst_ref, mat):
    m = mat.shape[0]
    for s in range(mat.shape[1] // LANES):
        dst_ref[pl.ds(s, m, stride=8), :] = mat[:, s * LANES:(s + 1) * LANES]


def _tile_row_chunk(src_ref, s, m):
    return src_ref[pl.ds(s, m, stride=8), :]


def _dispatch_kernel(pos_ref, hp_ref, xs_in_ref, xs_ref, sbuf, sem, *, n_tiles):
    del xs_in_ref
    i = pl.program_id(0)
    slot = i % 2

    def wait_slot(s):
        for _ in range(TOP_K):
            pltpu.make_async_copy(sbuf.at[s], xs_ref.at[pl.ds(0, ROW_TILE * 8)], sem.at[s]).wait()

    @pl.when(i >= 2)
    def _():
        wait_slot(slot)

    _to_tile_rows(sbuf.at[slot], hp_ref[...])

    def copies(r):
        return [pltpu.make_async_copy(
            sbuf.at[slot, pl.ds(pl.multiple_of(r * 8, 8), 8)],
            xs_ref.at[pl.ds(pl.multiple_of(pos_ref[0, 0, k * ROW_TILE + r] * 8, 8), 8)], sem.at[slot])
            for k in range(TOP_K)]

    _row_dma_loop(ROW_TILE, copies)

    @pl.when(i == n_tiles - 1)
    def _():
        wait_slot(slot)
        if n_tiles >= 2:
            wait_slot(1 - slot)


def _dispatch(hp, pos3, n_slots):
    t, half = hp.shape
    assert half == 8 * LANES, "a packed row must be exactly one (8, 128) tile"
    n_tiles = t // ROW_TILE
    xs0 = jnp.zeros((n_slots * 8, LANES), I32)
    return pl.pallas_call(
        functools.partial(_dispatch_kernel, n_tiles=n_tiles),
        grid=(n_tiles,),
        in_specs=[pl.BlockSpec((1, 1, TOP_K * ROW_TILE), lambda i: (i, 0, 0), memory_space=pltpu.SMEM),
                  pl.BlockSpec((ROW_TILE, half), lambda i: (i, 0)),
                  pl.BlockSpec(memory_space=pl.ANY)],
        out_specs=pl.BlockSpec(memory_space=pl.ANY),
        out_shape=jax.ShapeDtypeStruct((n_slots * 8, LANES), I32),
        scratch_shapes=[pltpu.VMEM((2, ROW_TILE * 8, LANES), I32), pltpu.SemaphoreType.DMA((2,))],
        input_output_aliases={2: 0},
        compiler_params=_cparams(("arbitrary",), VMEM_LIMIT),
        name="moe_dispatch",
    )(pos3, hp, xs0)


def _moe_kernel(be_ref, nu_ref, x_ref, wg_ref, wu_ref, wd_ref, y_ref):
    b = pl.program_id(0)
    half = wg_ref.shape[1] // 2
    n_used = nu_ref[0, 0]

    @pl.when(b < n_used)
    def _():
        u = jnp.concatenate([_tile_row_chunk(x_ref, s, MOE_BLK) for s in range(half // LANES)], axis=1)
        x_hi = _unpack_hi(u).astype(BF16)
        x_lo = _unpack_lo(u).astype(BF16)

        def proj(w_ref):
            return (jnp.dot(x_hi, w_ref[0, :half, :], preferred_element_type=F32)
                    + jnp.dot(x_lo, w_ref[0, half:, :], preferred_element_type=F32))

        act = (_silu(proj(wg_ref)) * proj(wu_ref)).astype(BF16)
        _to_tile_rows(y_ref, _pack_pairs(jnp.dot(act, wd_ref[0], preferred_element_type=F32)))

    @pl.when(b >= n_used)
    def _():
        y_ref[...] = jnp.zeros(y_ref.shape, y_ref.dtype)


def _moe(xs, be, n_used, wg, wu, wd):
    nb = xs.shape[0] // (MOE_BLK * 8)
    _, d, d_exp = wg.shape
    w_idx = lambda b, be, nu: (be[0, b], 0, 0)
    row_blk = pl.BlockSpec((MOE_BLK * 8, LANES), lambda b, be, nu: (b, 0))
    grid_spec = pltpu.PrefetchScalarGridSpec(
        num_scalar_prefetch=2,
        grid=(nb,),
        in_specs=[row_blk,
                  pl.BlockSpec((1, d, d_exp), w_idx),
                  pl.BlockSpec((1, d, d_exp), w_idx),
                  pl.BlockSpec((1, d_exp, d), w_idx)],
        out_specs=row_blk)
    return pl.pallas_call(
        _moe_kernel,
        grid_spec=grid_spec,
        out_shape=jax.ShapeDtypeStruct(xs.shape, I32),
        compiler_params=_cparams(("arbitrary",), VMEM_LIMIT),
        name="moe_experts",
    )(be, n_used, xs, wg, wu, wd)


def _combine_kernel(pos_ref, posn_ref, x_ref, mod_ref, gt_ref, y_hbm, o_ref, ybuf, sem, *, d, n_tiles):
    i = pl.program_id(0)
    slot = i % 2
    half = d // 2

    def issue(p_ref, s):
        def copies(r):
            return [pltpu.make_async_copy(
                y_hbm.at[pl.ds(pl.multiple_of(p_ref[0, 0, k * ROW_TILE + r] * 8, 8), 8)],
                ybuf.at[s, k, pl.ds(pl.multiple_of(r * 8, 8), 8)], sem.at[s])
                for k in range(TOP_K)]
        _row_dma_loop(ROW_TILE, copies)

    @pl.when(i == 0)
    def _():
        issue(pos_ref, 0)

    @pl.when(i + 1 < n_tiles)
    def _():
        issue(posn_ref, 1 - slot)

    for k in range(TOP_K):
        pltpu.make_async_copy(y_hbm.at[pl.ds(0, ROW_TILE * 8)], ybuf.at[slot, k], sem.at[slot]).wait()

    w0, w1 = gt_ref[:, 0:1], gt_ref[:, 1:2]
    for s in range(half // LANES):
        u0 = _tile_row_chunk(ybuf.at[slot, 0], s, ROW_TILE)
        u1 = _tile_row_chunk(ybuf.at[slot, 1], s, ROW_TILE)
        for off, unpack in ((0, _unpack_hi), (half, _unpack_lo)):
            lo, hi = off + s * LANES, off + (s + 1) * LANES
            f = w0 * unpack(u0) + w1 * unpack(u1)
            o_ref[:, lo:hi] = x_ref[:, lo:hi] + mod_ref[0, :, 5 * d + lo:5 * d + hi] * f


def _combine(x_mid, mod3, layer, gates, pos3, y, *, n_x_tiles, tiles_per_seq):
    t, d = x_mid.shape
    n_batch = n_x_tiles // tiles_per_seq
    n_tiles = t // ROW_TILE
    half = d // 2

    def mod_idx(i):
        return (layer * 8 + jnp.where(i < n_x_tiles, i // tiles_per_seq, n_batch), 0, 0)

    smem_blk = lambda f: pl.BlockSpec((1, 1, TOP_K * ROW_TILE), f, memory_space=pltpu.SMEM)
    return pl.pallas_call(
        functools.partial(_combine_kernel, d=d, n_tiles=n_tiles),
        grid=(n_tiles,),
        in_specs=[smem_blk(lambda i: (i, 0, 0)),
                  smem_blk(lambda i: (jnp.minimum(i + 1, n_tiles - 1), 0, 0)),
                  pl.BlockSpec((ROW_TILE, d), lambda i: (i, 0)),
                  pl.BlockSpec((1, 1, mod3.shape[2]), mod_idx),
                  pl.BlockSpec((ROW_TILE, TOP_K), lambda i: (i, 0)),
                  pl.BlockSpec(memory_space=pl.ANY)],
        out_specs=pl.BlockSpec((ROW_TILE, d), lambda i: (i, 0)),
        out_shape=jax.ShapeDtypeStruct((t, d), F32),
        scratch_shapes=[pltpu.VMEM((2, TOP_K, ROW_TILE * 8, LANES), I32), pltpu.SemaphoreType.DMA((2,))],
        compiler_params=_cparams(("arbitrary",), VMEM_LIMIT),
        name="moe_combine",
    )(pos3, pos3, x_mid, mod3, gates, y)


def _rope_tables(seq, tm):
    nf = HEAD_DIM // 4
    inv = np.power(np.float32(ROPE_BASE), -np.arange(nf, dtype=np.float32) / np.float32(nf)).astype(np.float32)
    tt = np.arange(seq)
    ar = (tt // GRID_W).astype(np.float32)[:, None] * inv
    ac = (tt % GRID_W).astype(np.float32)[:, None] * inv
    ang = np.concatenate([ar, ar, ac, ac], axis=-1)
    cos = np.concatenate([np.cos(ang), np.ones((tm, HEAD_DIM), np.float32)], axis=0)
    sin = np.concatenate([np.sin(ang), np.zeros((tm, HEAD_DIM), np.float32)], axis=0)
    return jnp.asarray(cos, F32), jnp.asarray(sin, F32)


def kernel(x, c, ctx, c_ctx, ada_w, ada_b, norm_mix_g, norm_ffn_g, ab_w_in, ab_w_out, ab_q_norm, ab_k_norm, ab_sink, conv_w, conv_b, conv_ln_g, conv_ln_b, na_w_in, na_w_out, na_q_norm, na_k_norm, na_rpb, router_w, router_b, moe_w_gate, moe_w_up, moe_w_down):
    batch, seq, d = x.shape
    ctx_len = ctx.shape[1]
    depth = ada_w.shape[0]
    n_exp = router_w.shape[1]
    b_ch = conv_w.shape[-1]
    a_qw = ab_w_out.shape[1] - b_ch
    a_kvw = (ab_w_in.shape[-1] - a_qw - 2 * b_ch) // 2
    n_kv = 2
    na_w = na_w_out.shape[1]
    assert batch + 1 <= 8 and a_qw == b_ch and n_exp % 8 == 0

    tx, tc = batch * seq, batch * ctx_len
    t = tx + tc
    tm_in = next(m for m in (1024, 512, 256) if seq % m == 0 and tc % m == 0)
    tn = 512 if d >= 2048 else 256

    c8 = jnp.zeros((8, d), F32).at[:batch].set(c).at[batch].set(c_ctx)
    mod3 = _ada_all(c8, ada_w, ada_b).reshape(depth * 8, 1, 6 * d)

    cos, sin = _rope_tables(seq, tm_in)
    rw = jnp.zeros((d, LANES), F32).at[:, :n_exp].set(router_w.astype(F32))
    rw_hi = rw.astype(BF16)
    rw_cat = jnp.concatenate([rw_hi, (rw - rw_hi.astype(F32)).astype(BF16)], axis=1)
    rows = seq // GRID_W

    k0, v0, u0 = a_qw, a_qw + a_kvw, a_qw + 2 * a_kvw
    k_off, v_off = a_qw + 2 * b_ch, a_qw + 2 * b_ch + a_kvw
    ab_w_in_b = jnp.concatenate([ab_w_in[..., :k0], ab_w_in[..., u0:], ab_w_in[..., k0:u0]], axis=-1).astype(BF16)
    ab_w_out_b = ab_w_out.astype(BF16)
    na_w_in_b = na_w_in.astype(BF16)
    na_w_out_b = na_w_out.astype(BF16)
    d_exp = moe_w_gate.shape[-1]
    wg_all = moe_w_gate.astype(BF16).reshape(depth * n_exp, d, d_exp)
    wu_all = moe_w_up.astype(BF16).reshape(depth * n_exp, d, d_exp)
    wd_all = moe_w_down.astype(BF16).reshape(depth * n_exp, d_exp, d)

    def ab_kind(col):
        return "q" if col < a_qw else ("k" if k_off <= col < v_off else "p")

    def na_kind(col):
        return "q" if col < na_w else ("k" if col < 2 * na_w else "p")

    tok = jnp.concatenate([x.reshape(tx, d), ctx.reshape(tc, d)], axis=0)

    for i in range(depth):
        with_ctx = i < depth - 1
        j = i // 2
        t_act = t if with_ctx else tx
        tiles = dict(n_x_tiles=tx // tm_in, tiles_per_seq=seq // tm_in)
        if i % 2 == 0:
            px = _inproj(tok, mod3, i, norm_mix_g[i], ab_w_in_b, j, ab_q_norm[j], ab_k_norm[j], ab_kind, cos, sin,
                         tm=tm_in, tn=tn, **tiles)
            att = _win_attn(px, ab_sink[j], batch=batch, seq=seq, ctx_len=ctx_len, a_qw=a_qw, n_kv=n_kv,
                            k_off=k_off, v_off=v_off, with_ctx=with_ctx)
            cv = _conv(px, conv_w[j], conv_b[j], conv_ln_g[j], conv_ln_b[j], a_off=a_qw, g_off=a_qw + b_ch,
                       batch=batch, seq=seq, ctx_len=ctx_len, with_ctx=with_ctx, tmc=256)
            acts, w_out = [att, cv], ab_w_out_b
        else:
            px = _inproj(tok, mod3, i, norm_mix_g[i], na_w_in_b, j, na_q_norm[j], na_k_norm[j], na_kind, None, None,
                         tm=tm_in, tn=tn, **tiles)
            bias = _na_bias_table(na_rpb[j], rows)
            att = _na_attn(px, bias, batch=batch, seq=seq, ctx_len=ctx_len, n_heads=na_w // HEAD_DIM,
                           with_ctx=with_ctx)
            acts, w_out = [att], na_w_out_b

        otiles = dict(n_x_tiles=tx // ROW_TILE, tiles_per_seq=seq // ROW_TILE)
        x_mid, hp, logits_t = _outproj(tok, mod3, i, norm_ffn_g[i], rw_cat, rw_hi, acts, w_out, j, t_act=t_act,
                                       n_exp=n_exp, tm=ROW_TILE, **otiles)
        g_out, pos3, be, n_used, nb = _route_plan(logits_t, router_b, i * n_exp)
        xs = _dispatch(hp, pos3, nb * MOE_BLK)
        y = _moe(xs, be, n_used, wg_all, wu_all, wd_all)
        tok = _combine(x_mid, mod3, i, g_out[:TOP_K].T, pos3, y, **otiles)

    return tok[:tx].reshape(batch, seq, d)
```
